```python
import math
import jax, jax.numpy as jnp
from jax import lax
import numpy as np

D_MODEL = 1024
BATCH = 8
SEQ = 2048
DEPTH = 1

GRID_W = 64
NA_HEADS = 8
NA_HEAD_DIM = 64
NA_WIDTH = NA_HEADS * NA_HEAD_DIM
NA_WIN_ROWS = 8
NA_WIN_COLS = 16
DIFF_HEADS = 4
DIFF_HEAD_DIM = 64
DIFF_V_DIM = 2 * DIFF_HEAD_DIM
DIFF_QK = DIFF_HEADS * 2 * DIFF_HEAD_DIM
DIFF_WIDTH = DIFF_HEADS * DIFF_V_DIM
MIX_WIDTH = NA_WIDTH + DIFF_WIDTH
IN_WIDTH = 3 * NA_WIDTH + 2 * DIFF_QK + DIFF_WIDTH
IN_SPLITS = (NA_WIDTH, 2 * NA_WIDTH, 3 * NA_WIDTH, 3 * NA_WIDTH + DIFF_QK, 3 * NA_WIDTH + 2 * DIFF_QK)
ATTN_BLOCK = 128
T5_BUCKETS = 32
T5_MAX_DIST = 128
PEER_HEADS = 8
PEER_NKEYS = 128
PEER_N = PEER_NKEYS * PEER_NKEYS
PEER_DKEY = 256
PEER_TOPK = 16
PEER_CHUNK = 128
LN_EPS = 1e-5
DN_ALPHA = (2.0 * DEPTH) ** 0.25
DN_BETA = (8.0 * DEPTH) ** -0.25
NEG_BIG = -1e30

kernel_name = "hybrid_natten_diffattn_peer_encoder"


def _layer_norm(x, gain=None, bias=None):
    xf = x.astype(jnp.float32)
    mu = jnp.mean(xf, axis=-1, keepdims=True)
    var = jnp.mean(jnp.square(xf - mu), axis=-1, keepdims=True)
    y = (xf - mu) * lax.rsqrt(var + LN_EPS)
    if gain is not None:
        y = y * gain.astype(jnp.float32) + bias.astype(jnp.float32)
    return y.astype(x.dtype)


def _t5_bucket(rel):
    nb = T5_BUCKETS // 2
    max_exact = nb // 2
    bucket = jnp.where(rel > 0, nb, 0)
    n = jnp.abs(rel)
    nf = jnp.maximum(n, 1).astype(jnp.float32)
    large = max_exact + (jnp.log(nf / max_exact) / math.log(T5_MAX_DIST / max_exact)
                         * (nb - max_exact)).astype(jnp.int32)
    large = jnp.minimum(large, nb - 1)
    return bucket + jnp.where(n < max_exact, n, large)


def _neighbourhood_attention(q, k, v, rpb):
    b, s, h, dh = q.shape
    rows = s // GRID_W
    kr = min(NA_WIN_ROWS, rows)
    kc = NA_WIN_COLS
    r = jnp.arange(rows)
    row_start = jnp.clip(r - kr // 2, 0, rows - kr)
    row_idx = row_start[:, None] + jnp.arange(kr)[None, :]
    col = jnp.arange(GRID_W)
    col_start = jnp.clip(col - kc // 2, 0, GRID_W - kc)
    col_valid = (col[None, :] >= col_start[:, None]) & (col[None, :] < col_start[:, None] + kc)
    row_off = row_idx - r[:, None] + (NA_WIN_ROWS - 1)
    col_off = jnp.clip(col[None, :] - col[:, None], -(kc - 1), kc - 1) + (NA_WIN_COLS - 1)
    bias = rpb[:, row_off[:, None, :, None], col_off[None, :, None, :]].astype(jnp.float32)
    bias = jnp.where(col_valid[None, None, :, None, :], bias, NEG_BIG)
    qg = q.reshape(b, rows, GRID_W, h, dh)
    kg = k.reshape(b, rows, GRID_W, h, dh)[:, row_idx]
    vg = v.reshape(b, rows, GRID_W, h, dh)[:, row_idx]
    logits = jnp.einsum('brqhd,brikhd->bhrqik', qg, kg,
                        preferred_element_type=jnp.float32) * (dh ** -0.5) + bias[None]
    p = jax.nn.softmax(logits.reshape(b, h, rows, GRID_W, kr * GRID_W), axis=-1)
    p = p.reshape(b, h, rows, GRID_W, kr, GRID_W).astype(v.dtype)
    o = jnp.einsum('bhrqik,brikhd->brqhd', p, vg)
    return o.reshape(b, s, h * dh)


def _diff_attention(q, k, v, t5_bias, lq1, lk1, lq2, lk2, subln_g, lambda_init):
    b, s, h, _, d = q.shape
    f32 = jnp.float32
    lam = (jnp.exp(jnp.sum(lq1.astype(f32) * lk1.astype(f32)))
           - jnp.exp(jnp.sum(lq2.astype(f32) * lk2.astype(f32))) + lambda_init)
    nblk = s // ATTN_BLOCK
    qb = jnp.moveaxis(q.reshape(b, nblk, ATTN_BLOCK, h, 2, d), 1, 0)
    kpos = jnp.arange(s)

    def block(args):
        qi, i = args
        qpos = i * ATTN_BLOCK + jnp.arange(ATTN_BLOCK)
        bucket = _t5_bucket(kpos[None, :] - qpos[:, None])
        bias = jnp.transpose(t5_bias[bucket].astype(f32), (2, 0, 1))
        logits = jnp.einsum('bqhmd,bkhmd->bhmqk', qi, k,
                            preferred_element_type=f32) * (d ** -0.5) + bias[None, :, None]
        p = jax.nn.softmax(logits, axis=-1)
        w = (p[:, :, 0] - lam * p[:, :, 1]).astype(v.dtype)
        return jnp.einsum('bhqk,bkhe->bqhe', w, v)

    o = lax.map(block, (qb, jnp.arange(nblk)))
    o = jnp.moveaxis(o, 0, 1).reshape(b, s, h, 2 * d).astype(f32)
    o = o * lax.rsqrt(jnp.mean(o * o, axis=-1, keepdims=True) + LN_EPS) * subln_g.astype(f32)
    return (o * (1.0 - lambda_init)).astype(v.dtype).reshape(b, s, h * 2 * d)


def _peer(u, w_query, sub_keys, expert_down, expert_up):
    b, s, dm = u.shape
    tokens = u.reshape(-1, PEER_CHUNK, dm)

    def chunk(xc):
        qc = (xc @ w_query).reshape(PEER_CHUNK, PEER_HEADS, 2, PEER_DKEY // 2)
        s_half = jnp.einsum('chpk,pnk->chpn', qc, sub_keys,
                            preferred_element_type=jnp.float32)
        top_s, top_i = lax.top_k(s_half, PEER_TOPK)
        cand_s = (top_s[:, :, 0, :, None] + top_s[:, :, 1, None, :]).reshape(PEER_CHUNK, PEER_HEADS, -1)
        cand_i = (top_i[:, :, 0, :, None] * PEER_NKEYS + top_i[:, :, 1, None, :]).reshape(PEER_CHUNK, PEER_HEADS, -1)
        best_s, best_j = lax.top_k(cand_s, PEER_TOPK)
        expert = jnp.take_along_axis(cand_i, best_j, axis=-1)
        g = jax.nn.softmax(best_s, axis=-1)
        u_e = expert_down[expert]
        act = jax.nn.gelu(jnp.einsum('chkd,cd->chk', u_e, xc,
                                     preferred_element_type=jnp.float32), approximate=False)
        v_e = expert_up[expert]
        return jnp.einsum('chk,chkd->cd', (g * act).astype(xc.dtype), v_e)

    out = lax.map(chunk, tokens)
    return out.reshape(b, s, dm)


def setup_inputs(seed: int = 0) -> dict:
    key = jax.random.key(seed)
    ks = jax.random.split(key, 24)
    nrm = jax.random.normal
    f32 = jnp.float32
    d = D_MODEL
    return {
        "x": nrm(ks[0], (BATCH, SEQ, d), f32),
        "c": nrm(ks[1], (BATCH, d), f32),
        "w_ada": nrm(ks[2], (DEPTH, d, 6 * d), f32) * (0.2 * d ** -0.5),
        "b_ada": nrm(ks[3], (DEPTH, 6 * d), f32) * 0.01,
        "w_in": nrm(ks[4], (DEPTH, d, IN_WIDTH), f32) * d ** -0.5,
        "w_out": nrm(ks[5], (DEPTH, MIX_WIDTH, d), f32) * (MIX_WIDTH ** -0.5 * DN_BETA),
        "na_rpb": nrm(ks[6], (DEPTH, NA_HEADS, 2 * NA_WIN_ROWS - 1, 2 * NA_WIN_COLS - 1), f32) * 0.1,
        "t5_bias": nrm(ks[7], (T5_BUCKETS, DIFF_HEADS), f32) * 0.1,
        "lambda_q1": nrm(ks[8], (DEPTH, DIFF_HEAD_DIM), f32) * 0.1,
        "lambda_k1": nrm(ks[9], (DEPTH, DIFF_HEAD_DIM), f32) * 0.1,
        "lambda_q2": nrm(ks[10], (DEPTH, DIFF_HEAD_DIM), f32) * 0.1,
        "lambda_k2": nrm(ks[11], (DEPTH, DIFF_HEAD_DIM), f32) * 0.1,
        "diff_subln": 1.0 + 0.02 * nrm(ks[12], (DEPTH, DIFF_V_DIM), f32),
        "ln1_g": 1.0 + 0.02 * nrm(ks[13], (DEPTH, d), f32),
        "ln1_b": 0.02 * nrm(ks[14], (DEPTH, d), f32),
        "w_query": nrm(ks[15], (DEPTH, d, PEER_HEADS * PEER_DKEY), f32) * d ** -0.5,
        "sub_keys": nrm(ks[16], (DEPTH, 2, PEER_NKEYS, PEER_DKEY // 2), f32) * (PEER_DKEY // 2) ** -0.5,
        "expert_down": nrm(ks[17], (DEPTH, PEER_N, d), f32) * d ** -0.5,
        "expert_up": nrm(ks[18], (DEPTH, PEER_N, d), f32) * DN_BETA,
        "ln2_g": 1.0 + 0.02 * nrm(ks[19], (DEPTH, d), f32),
        "ln2_b": 0.02 * nrm(ks[20], (DEPTH, d), f32),
    }


def reference(x, c, w_ada, b_ada, w_in, w_out, na_rpb, t5_bias, lambda_q1, lambda_k1,
              lambda_q2, lambda_k2, diff_subln, ln1_g, ln1_b, w_query, sub_keys,
              expert_down, expert_up, ln2_g, ln2_b):
    b, s, _ = x.shape
    for l in range(DEPTH):
        lambda_init = 0.8 - 0.6 * math.exp(-0.3 * l)
        mod = jnp.einsum('bd,de->be', jax.nn.silu(c), w_ada[l]) + b_ada[l]
        sh_a, sc_a, g_a, sh_f, sc_f, g_f = jnp.split(mod[:, None, :], 6, axis=-1)

        h = _layer_norm(x) * (1.0 + sc_a) + sh_a
        proj = h @ w_in[l]
        na_q, na_k, na_v, d_q, d_k, d_v = jnp.split(proj, IN_SPLITS, axis=-1)
        o_na = _neighbourhood_attention(
            na_q.reshape(b, s, NA_HEADS, NA_HEAD_DIM),
            na_k.reshape(b, s, NA_HEADS, NA_HEAD_DIM),
            na_v.reshape(b, s, NA_HEADS, NA_HEAD_DIM),
            na_rpb[l])
        o_diff = _diff_attention(
            d_q.reshape(b, s, DIFF_HEADS, 2, DIFF_HEAD_DIM),
            d_k.reshape(b, s, DIFF_HEADS, 2, DIFF_HEAD_DIM),
            d_v.reshape(b, s, DIFF_HEADS, DIFF_V_DIM),
            t5_bias, lambda_q1[l], lambda_k1[l], lambda_q2[l], lambda_k2[l],
            diff_subln[l], lambda_init)
        mix = jnp.concatenate([o_na, o_diff], axis=-1) @ w_out[l]
        x = _layer_norm(DN_ALPHA * x + (1.0 + g_a) * mix, ln1_g[l], ln1_b[l])

        h = _layer_norm(x) * (1.0 + sc_f) + sh_f
        f = _peer(h, w_query[l], sub_keys[l], expert_down[l], expert_up[l])
        x = _layer_norm(DN_ALPHA * x + (1.0 + g_f) * f, ln2_g[l], ln2_b[l])
    return x
```

```python
import functools
import math

import jax
import jax.numpy as jnp
from jax import lax
from jax.experimental import pallas as pl
from jax.experimental.pallas import tpu as pltpu

F32 = jnp.float32
BF16 = jnp.bfloat16

D_MODEL = 1024
DEPTH = 1
GRID_W = 64
NA_HEADS = 8
NA_HEAD_DIM = 64
NA_WIDTH = NA_HEADS * NA_HEAD_DIM
NA_WIN_ROWS = 8
NA_WIN_COLS = 16
DIFF_HEADS = 4
DIFF_HEAD_DIM = 64
DIFF_V_DIM = 2 * DIFF_HEAD_DIM
DIFF_WIDTH = DIFF_HEADS * DIFF_V_DIM
IN_WIDTH = 3 * NA_WIDTH + 3 * DIFF_WIDTH
T5_BUCKETS = 32
T5_MAX_DIST = 128
PEER_HEADS = 8
PEER_NKEYS = 128
PEER_N = PEER_NKEYS * PEER_NKEYS
PEER_TOPK = 16
LN_EPS = 1e-5
DN_ALPHA = (2.0 * DEPTH) ** 0.25
NEG_BIG = -1e30
LANES = 128

PROJ_TM = 256
NA_QROWS = 4
NA_KROWS = 12
DIFF_TQ = 256
MID_TM = 256
TOPK_TL = 512
PEER_TM = 512
PEER_TE = 1024
VMEM_LIMIT = 56 * 1024 * 1024

_NT = (((1,), (1,)), ((), ()))


def _cparams(*sem):
    return pltpu.CompilerParams(dimension_semantics=sem, vmem_limit_bytes=VMEM_LIMIT)


def _ln(x):
    mu = jnp.mean(x, axis=-1, keepdims=True)
    xc = x - mu
    var = jnp.mean(xc * xc, axis=-1, keepdims=True)
    return xc * lax.rsqrt(var + LN_EPS)


def _ada_body(c_ref, w_ref, b_ref, o_ref):
    c = c_ref[...]
    s = c * jax.nn.sigmoid(c)
    o_ref[...] = jnp.dot(s, w_ref[...], preferred_element_type=F32) + b_ref[...]


def _ada(c, w, bias):
    b, d = c.shape
    n = w.shape[1]
    return pl.pallas_call(
        _ada_body,
        grid=(n // d,),
        in_specs=[pl.BlockSpec((b, d), lambda j: (0, 0)),
                  pl.BlockSpec((d, d), lambda j: (0, j)),
                  pl.BlockSpec((1, d), lambda j: (0, j))],
        out_specs=pl.BlockSpec((b, d), lambda j: (0, j)),
        out_shape=jax.ShapeDtypeStruct((b, n), F32),
        compiler_params=_cparams("arbitrary"),
        name="ada",
    )(c, w, bias)


def _proj_body(x_ref, sc_ref, sh_ref, w_ref, o_ref):
    h = _ln(x_ref[...]) * (1.0 + sc_ref[...]) + sh_ref[...]
    o_ref[...] = jnp.dot(h.astype(BF16), w_ref[...], preferred_element_type=F32).astype(o_ref.dtype)


def _proj(xf, mod4, w, seq):
    t, d = xf.shape
    n = w.shape[1]
    tpb = seq // PROJ_TM
    return pl.pallas_call(
        _proj_body,
        grid=(t // PROJ_TM,),
        in_specs=[pl.BlockSpec((PROJ_TM, d), lambda i: (i, 0)),
                  pl.BlockSpec((None, None, 1, d), lambda i: (i // tpb, 1, 0, 0)),
                  pl.BlockSpec((None, None, 1, d), lambda i: (i // tpb, 0, 0, 0)),
                  pl.BlockSpec((d, n), lambda i: (0, 0))],
        out_specs=pl.BlockSpec((PROJ_TM, n), lambda i: (i, 0)),
        out_shape=jax.ShapeDtypeStruct((t, n), BF16),
        compiler_params=_cparams("arbitrary"),
        name="proj",
    )(xf, mod4, mod4, w)


def _na_row_offsets(rows):
    kr = min(NA_WIN_ROWS, rows)
    nrb = rows // NA_QROWS
    pats = []
    for rb in range(nrb):
        r0 = rb * NA_QROWS
        s = min(max(r0 - NA_WIN_ROWS // 2, 0), rows - NA_KROWS)
        pat = []
        for qr in range(NA_QROWS):
            r = r0 + qr
            rs = min(max(r - kr // 2, 0), rows - kr)
            pat.append(tuple((s + i) - r + (NA_WIN_ROWS - 1) if rs <= s + i < rs + kr else 15
                             for i in range(NA_KROWS)))
        pats.append(tuple(pat))
    variants = [pats[0], pats[1], pats[-1]]
    assert all(p == variants[1] for p in pats[1:-1])
    return variants


def _na_bias_body(rpb_ref, o_ref, tile_ref, *, variants):
    h = pl.program_id(0)
    nro = 2 * NA_WIN_ROWS - 1
    nco = 2 * NA_WIN_COLS - 1
    q = lax.broadcasted_iota(jnp.int32, (GRID_W, LANES), 0)
    kk = lax.broadcasted_iota(jnp.int32, (GRID_W, LANES), 1)
    k = jnp.bitwise_and(kk, GRID_W - 1)
    col_start = jnp.clip(q - NA_WIN_COLS // 2, 0, GRID_W - NA_WIN_COLS)
    valid = jnp.logical_and(k >= col_start, k < col_start + NA_WIN_COLS)
    co = jnp.clip(k - q, -(NA_WIN_COLS - 1), NA_WIN_COLS - 1) + (NA_WIN_COLS - 1)
    neg = jnp.full((GRID_W, LANES), NEG_BIG, F32)
    for ro in range(nro):
        acc = neg
        for cc in range(nco):
            acc = jnp.where(co == cc, rpb_ref[(h * nro + ro) * nco + cc], acc)
        tile_ref[ro] = jnp.where(valid, acc, NEG_BIG)
    tile_ref[nro] = neg
    left = kk < GRID_W
    for v, pat in enumerate(variants):
        for qr in range(NA_QROWS):
            for kp in range(NA_KROWS // 2):
                a, b = pat[qr][2 * kp], pat[qr][2 * kp + 1]
                o_ref[v, qr * GRID_W:(qr + 1) * GRID_W, kp * LANES:(kp + 1) * LANES] = (
                    jnp.where(left, tile_ref[a], tile_ref[b]))


def _na_bias(rpb_flat, rows):
    variants = _na_row_offsets(rows)
    nq, nk = NA_QROWS * GRID_W, NA_KROWS * GRID_W
    return pl.pallas_call(
        functools.partial(_na_bias_body, variants=variants),
        grid=(NA_HEADS,),
        in_specs=[pl.BlockSpec(memory_space=pltpu.SMEM)],
        out_specs=pl.BlockSpec((3, None, nq, nk), lambda h: (0, h, 0, 0)),
        out_shape=jax.ShapeDtypeStruct((3, NA_HEADS, nq, nk), F32),
        scratch_shapes=[pltpu.VMEM((2 * NA_WIN_ROWS, GRID_W, LANES), F32)],
        compiler_params=_cparams("arbitrary"),
        name="na_bias",
    )(rpb_flat)


def _half_masks():
    lane = lax.broadcasted_iota(jnp.int32, (1, LANES), 1)
    lo = jnp.where(lane < LANES // 2, 1.0, 0.0).astype(BF16)
    return lo, (1.0 - lo.astype(F32)).astype(BF16)


def _na_body(q_ref, k_ref, v_ref, b_ref, o_ref, *, rows):
    rb = pl.program_id(2)
    s_row = jnp.clip(rb * NA_QROWS - NA_WIN_ROWS // 2, 0, rows - NA_KROWS)
    start = pl.multiple_of(s_row * GRID_W, GRID_W)
    nk = NA_KROWS * GRID_W
    q = q_ref[...]
    kw = k_ref[pl.ds(start, nk), :]
    vw = v_ref[pl.ds(start, nk), :]
    acc = jnp.zeros(q.shape, F32)
    for h, m in enumerate(_half_masks()):
        s = lax.dot_general(q * m, kw, _NT, preferred_element_type=F32)
        s = s * (NA_HEAD_DIM ** -0.5) + b_ref[h]
        p = jnp.exp(s - jnp.max(s, axis=-1, keepdims=True))
        l = jnp.sum(p, axis=-1, keepdims=True)
        o = jnp.dot(p.astype(BF16), vw * m, preferred_element_type=F32)
        acc = acc + o * (1.0 / l)
    o_ref[...] = acc.astype(o_ref.dtype)


def _na(proj3, nb, rows):
    b, seq, _ = proj3.shape
    nrb = rows // NA_QROWS
    nq, nk = NA_QROWS * GRID_W, NA_KROWS * GRID_W
    npair = NA_HEADS // 2

    def variant(rb):
        return jnp.where(rb == 0, 0, jnp.where(rb == nrb - 1, 2, 1))

    return pl.pallas_call(
        functools.partial(_na_body, rows=rows),
        grid=(b, npair, nrb),
        in_specs=[pl.BlockSpec((None, nq, LANES), lambda bi, hp, rb: (bi, rb, hp)),
                  pl.BlockSpec((None, seq, LANES), lambda bi, hp, rb: (bi, 0, npair + hp)),
                  pl.BlockSpec((None, seq, LANES), lambda bi, hp, rb: (bi, 0, 2 * npair + hp)),
                  pl.BlockSpec((None, 2, nq, nk), lambda bi, hp, rb: (variant(rb), hp, 0, 0))],
        out_specs=pl.BlockSpec((None, nq, LANES), lambda bi, hp, rb: (bi, rb, hp)),
        out_shape=jax.ShapeDtypeStruct((b, seq, NA_WIDTH), BF16),
        compiler_params=_cparams("arbitrary", "arbitrary", "arbitrary"),
        name="na",
    )(proj3, proj3, proj3, nb)


def _t5_bucket_of(rel):
    nb = T5_BUCKETS // 2
    max_exact = nb // 2
    bucket = jnp.where(rel > 0, nb, 0)
    n = jnp.abs(rel)
    nf = jnp.maximum(n, 1).astype(jnp.float32)
    large = max_exact + (jnp.log(nf / max_exact) / math.log(T5_MAX_DIST / max_exact)
                         * (nb - max_exact)).astype(jnp.int32)
    large = jnp.minimum(large, nb - 1)
    return bucket + jnp.where(n < max_exact, n, large)


def _t5_bias_body(bkt_ref, t5_ref, o_ref):
    h = pl.program_id(0)
    bkt = bkt_ref[...]
    acc = jnp.zeros(bkt.shape, F32)
    for bb in range(T5_BUCKETS):
        acc = jnp.where(bkt == bb, t5_ref[bb * DIFF_HEADS + h], acc)
    o_ref[...] = acc


def _t5_bias(t5_flat, seq):
    nq = seq // DIFF_TQ
    nt = 2 * nq - 1
    d = jnp.arange(nt, dtype=jnp.int32)[:, None, None] - (nq - 1)
    qq = jnp.arange(DIFF_TQ, dtype=jnp.int32)[None, :, None]
    kk = jnp.arange(DIFF_TQ, dtype=jnp.int32)[None, None, :]
    bkt = _t5_bucket_of(d * DIFF_TQ + kk - qq)
    return pl.pallas_call(
        _t5_bias_body,
        grid=(DIFF_HEADS, nt),
        in_specs=[pl.BlockSpec((None, DIFF_TQ, DIFF_TQ), lambda h, t: (t, 0, 0)),
                  pl.BlockSpec(memory_space=pltpu.SMEM)],
        out_specs=pl.BlockSpec((None, None, DIFF_TQ, DIFF_TQ), lambda h, t: (h, t, 0, 0)),
        out_shape=jax.ShapeDtypeStruct((DIFF_HEADS, nt, DIFF_TQ, DIFF_TQ), F32),
        compiler_params=_cparams("arbitrary", "arbitrary"),
        name="t5_bias",
    )(bkt, t5_flat)


def _diff_body(q_ref, k_ref, v_ref, dt_ref, lq1_ref, lk1_ref, lq2_ref, lk2_ref, g_ref, o_ref,
               *, nq, lambda_init):
    qi = pl.program_id(2)
    q = q_ref[...]
    lam = (jnp.exp(jnp.sum(lq1_ref[...] * lk1_ref[...], axis=-1, keepdims=True))
           - jnp.exp(jnp.sum(lq2_ref[...] * lk2_ref[...], axis=-1, keepdims=True)) + lambda_init)
    maps = []
    for m in _half_masks():
        qm = q * m
        pieces = []
        for kb in range(nq):
            s = lax.dot_general(qm, k_ref[kb * DIFF_TQ:(kb + 1) * DIFF_TQ, :], _NT,
                                preferred_element_type=F32)
            pieces.append(s * (DIFF_HEAD_DIM ** -0.5) + dt_ref[nq - 1 - qi + kb])
        s = jnp.concatenate(pieces, axis=1)
        p = jnp.exp(s - jnp.max(s, axis=-1, keepdims=True))
        maps.append((p, jnp.sum(p, axis=-1, keepdims=True)))
    (p1, l1), (p2, l2) = maps
    w = p1 * (1.0 / l1) - p2 * (lam / l2)
    o = jnp.dot(w.astype(BF16), v_ref[...], preferred_element_type=F32)
    o = o * lax.rsqrt(jnp.mean(o * o, axis=-1, keepdims=True) + LN_EPS) * g_ref[...]
    o_ref[...] = (o * (1.0 - lambda_init)).astype(o_ref.dtype)


def _diff(proj3, dt, lq1, lk1, lq2, lk2, subln, lambda_init):
    b, seq, _ = proj3.shape
    nq = seq // DIFF_TQ
    nt = 2 * nq - 1
    c0 = 3 * NA_WIDTH // LANES
    vec = lambda n: pl.BlockSpec((1, n), lambda h, bi, qi: (0, 0))
    return pl.pallas_call(
        functools.partial(_diff_body, nq=nq, lambda_init=lambda_init),
        grid=(DIFF_HEADS, b, nq),
        in_specs=[pl.BlockSpec((None, DIFF_TQ, LANES), lambda h, bi, qi: (bi, qi, c0 + h)),
                  pl.BlockSpec((None, seq, LANES), lambda h, bi, qi: (bi, 0, c0 + DIFF_HEADS + h)),
                  pl.BlockSpec((None, seq, LANES), lambda h, bi, qi: (bi, 0, c0 + 2 * DIFF_HEADS + h)),
                  pl.BlockSpec((None, nt, DIFF_TQ, DIFF_TQ), lambda h, bi, qi: (h, 0, 0, 0)),
                  vec(DIFF_HEAD_DIM), vec(DIFF_HEAD_DIM), vec(DIFF_HEAD_DIM), vec(DIFF_HEAD_DIM),
                  vec(DIFF_V_DIM)],
        out_specs=pl.BlockSpec((None, DIFF_TQ, LANES), lambda h, bi, qi: (bi, qi, h)),
        out_shape=jax.ShapeDtypeStruct((b, seq, DIFF_WIDTH), BF16),
        compiler_params=_cparams("arbitrary", "arbitrary", "arbitrary"),
        name="diff",
    )(proj3, proj3, proj3, dt, lq1, lk1, lq2, lk2, subln)


def _mid_body(ona_ref, odf_ref, wout_ref, x_ref, ga_ref, scf_ref, shf_ref, g1_ref, b1_ref,
              wq_ref, sk_ref, x1_ref, h2_ref, st_ref):
    mix = (jnp.dot(ona_ref[...], wout_ref[0:NA_WIDTH, :], preferred_element_type=F32)
           + jnp.dot(odf_ref[...], wout_ref[NA_WIDTH:, :], preferred_element_type=F32))
    y = DN_ALPHA * x_ref[...] + (1.0 + ga_ref[...]) * mix
    x1 = _ln(y) * g1_ref[...] + b1_ref[...]
    x1_ref[...] = x1
    h2 = (_ln(x1) * (1.0 + scf_ref[...]) + shf_ref[...]).astype(BF16)
    h2_ref[...] = h2
    qp = jnp.dot(h2, wq_ref[...], preferred_element_type=F32).astype(BF16)
    for hp in range(2 * PEER_HEADS):
        st_ref[hp] = lax.dot_general(sk_ref[hp % 2], qp[:, hp * PEER_NKEYS:(hp + 1) * PEER_NKEYS],
                                     _NT, preferred_element_type=F32)


def _mid(o_na, o_df, w_out, xf, mod4, ln_g, ln_b, w_query, sub_keys, seq):
    t, d = xf.shape
    tpb = seq // MID_TM
    nqk = w_query.shape[1]
    modspec = lambda j: pl.BlockSpec((None, None, 1, d), lambda i: (i // tpb, j, 0, 0))
    full = lambda *s: pl.BlockSpec(s, lambda i: (0,) * len(s))
    return pl.pallas_call(
        _mid_body,
        grid=(t // MID_TM,),
        in_specs=[pl.BlockSpec((MID_TM, NA_WIDTH), lambda i: (i, 0)),
                  pl.BlockSpec((MID_TM, DIFF_WIDTH), lambda i: (i, 0)),
                  full(d, d),
                  pl.BlockSpec((MID_TM, d), lambda i: (i, 0)),
                  modspec(2), modspec(4), modspec(3),
                  full(1, d), full(1, d),
                  full(d, nqk),
                  full(2, PEER_NKEYS, PEER_NKEYS)],
        out_specs=[pl.BlockSpec((MID_TM, d), lambda i: (i, 0)),
                   pl.BlockSpec((MID_TM, d), lambda i: (i, 0)),
                   pl.BlockSpec((2 * PEER_HEADS, PEER_NKEYS, MID_TM), lambda i: (0, 0, i))],
        out_shape=[jax.ShapeDtypeStruct((t, d), F32),
                   jax.ShapeDtypeStruct((t, d), BF16),
                   jax.ShapeDtypeStruct((2 * PEER_HEADS, PEER_NKEYS, t), F32)],
        compiler_params=_cparams("arbitrary"),
        name="mid",
    )(o_na, o_df, w_out, xf, mod4, mod4, mod4, ln_g, ln_b, w_query, sub_keys)


def _extract_top(work, order, out_ref=None):
    rank = jnp.full(work.shape, float(PEER_TOPK), F32)
    big = float(PEER_N)
    for r in range(PEER_TOPK):
        m = jnp.max(work, axis=0, keepdims=True)
        first = jnp.min(jnp.where(work == m, order, big), axis=0, keepdims=True)
        sel = order == first
        rank = jnp.where(sel, float(r), rank)
        work = jnp.where(sel, -jnp.inf, work)
        if out_ref is not None:
            out_ref[r:r + 1, :] = m
    return rank


def _pair_cells(op, u1, u2):
    g = [op(u1[0:1], u2[0:8]), op(u1[0:1], u2[8:16])]
    for a in range(1, 8):
        g.append(op(u1[a:a + 1], u2[0:8]))
    g.append(op(u1[8:16], u2[0:1]))
    return jnp.concatenate(g, axis=0)


def _topk_body(st_ref, lc_ref, f1_ref, r2_ref, f2_ref, t1_ref, t2_ref, l_ref):
    tl = st_ref.shape[-1]
    key_id = lax.broadcasted_iota(jnp.int32, (PEER_NKEYS, tl), 0).astype(F32)
    s1 = st_ref[0]
    s2 = st_ref[1]
    rank1 = _extract_top(s1, key_id, t1_ref)
    rank2 = _extract_top(s2, key_id, t2_ref)
    t1 = t1_ref[...]
    t2 = t2_ref[...]
    cand = _pair_cells(jnp.add, t1, t2)
    row = lax.broadcasted_iota(jnp.int32, cand.shape, 0)
    mid = row - 16
    cell_id = jnp.where(row < 16, row,
                        jnp.where(row < 72,
                                  (jnp.right_shift(mid, 3) + 1) * 16 + jnp.bitwise_and(mid, 7),
                                  (row - 64) * 16)).astype(F32)
    crank = _extract_top(cand, cell_id)
    chosen = jnp.where(crank < float(PEER_TOPK), 1.0, 0.0)
    e1 = jnp.exp(t1 - t1[0:1])
    e2 = jnp.exp(t2 - t2[0:1])
    z = jnp.sum(chosen * _pair_cells(jnp.multiply, e1, e2), axis=0, keepdims=True)
    inv_z = 1.0 / z
    l_ref[0:1, :] = jnp.sum(chosen[0:16], axis=0, keepdims=True)
    for a in range(1, 8):
        l_ref[a:a + 1, :] = jnp.sum(chosen[8 + 8 * a:16 + 8 * a], axis=0, keepdims=True)
    l_ref[8:16, :] = chosen[72:80]
    lcnt = l_ref[...]
    lc = jnp.zeros((PEER_NKEYS, tl), F32)
    for a in range(PEER_TOPK):
        lc = jnp.where(rank1 == float(a), lcnt[a:a + 1], lc)
    lc_ref[...] = lc
    f1_ref[...] = jnp.exp(s1 - t1[0:1]) * inv_z
    r2_ref[...] = rank2
    f2_ref[...] = jnp.where(rank2 < float(PEER_TOPK), jnp.exp(s2 - t2[0:1]), 0.0)


def _topk(st):
    _, nk, t = st.shape
    tl = min(TOPK_TL, t)
    spec = pl.BlockSpec((None, nk, tl), lambda i, h: (h, 0, i))
    shp = jax.ShapeDtypeStruct((PEER_HEADS, nk, t), F32)
    return pl.pallas_call(
        _topk_body,
        grid=(t // tl, PEER_HEADS),
        in_specs=[pl.BlockSpec((2, nk, tl), lambda i, h: (h, 0, i))],
        out_specs=[spec, spec, spec, spec],
        out_shape=[shp, shp, shp, shp],
        scratch_shapes=[pltpu.VMEM((PEER_TOPK, tl), F32), pltpu.VMEM((PEER_TOPK, tl), F32),
                        pltpu.VMEM((PEER_TOPK, tl), F32)],
        compiler_params=_cparams("arbitrary", "arbitrary"),
        name="topk",
    )(st)


def _peer_body(h2_ref, ed_ref, eut_ref, lc_ref, f1_ref, r2_ref, f2_ref, x1_ref, gf_ref,
               g2_ref, b2_ref, o_ref, acc_ref, *, ni):
    et = pl.program_id(1)

    @pl.when(et == 0)
    def _():
        acc_ref[...] = jnp.zeros(acc_ref.shape, F32)

    at = lax.dot_general(ed_ref[...], h2_ref[...], _NT, preferred_element_type=F32)
    act = 0.5 * at * (1.0 + lax.erf(at * (2.0 ** -0.5)))
    gates = []
    for ii in range(ni):
        g = jnp.zeros((PEER_NKEYS, at.shape[1]), F32)
        for h in range(PEER_HEADS):
            keep = r2_ref[h] < lc_ref[h, ii:ii + 1, :]
            g = g + jnp.where(keep, f2_ref[h], 0.0) * f1_ref[h, ii:ii + 1, :]
        gates.append(g)
    wt = (jnp.concatenate(gates, axis=0) * act).astype(BF16)
    acc_ref[...] += jnp.dot(eut_ref[...], wt, preferred_element_type=F32)

    @pl.when(et == pl.num_programs(1) - 1)
    def _():
        f = acc_ref[...].T
        y = DN_ALPHA * x1_ref[...] + (1.0 + gf_ref[...]) * f
        o_ref[...] = _ln(y) * g2_ref[...] + b2_ref[...]


def _peer(h2, ed, eut, lc, f1, r2, f2, x1, mod4, ln_g, ln_b, seq):
    t, d = h2.shape
    tm = min(PEER_TM, t)
    te = PEER_TE
    ni = te // PEER_NKEYS
    ng = PEER_NKEYS // ni
    tpb = seq // tm
    lc4 = lc.reshape(PEER_HEADS, ng, ni, t)
    f14 = f1.reshape(PEER_HEADS, ng, ni, t)
    rowspec = pl.BlockSpec((PEER_HEADS, None, ni, tm), lambda i, e: (0, e, 0, i))
    colspec = pl.BlockSpec((PEER_HEADS, PEER_NKEYS, tm), lambda i, e: (0, 0, i))
    full = lambda *s: pl.BlockSpec(s, lambda i, e: (0,) * len(s))
    return pl.pallas_call(
        functools.partial(_peer_body, ni=ni),
        grid=(t // tm, PEER_N // te),
        in_specs=[pl.BlockSpec((tm, d), lambda i, e: (i, 0)),
                  pl.BlockSpec((te, d), lambda i, e: (e, 0)),
                  pl.BlockSpec((d, te), lambda i, e: (0, e)),
                  rowspec, rowspec, colspec, colspec,
                  pl.BlockSpec((tm, d), lambda i, e: (i, 0)),
                  pl.BlockSpec((None, None, 1, d), lambda i, e: (i // tpb, 5, 0, 0)),
                  full(1, d), full(1, d)],
        out_specs=pl.BlockSpec((tm, d), lambda i, e: (i, 0)),
        out_shape=jax.ShapeDtypeStruct((t, d), F32),
        scratch_shapes=[pltpu.VMEM((d, tm), F32)],
        compiler_params=_cparams("arbitrary", "arbitrary"),
        name="peer",
    )(h2, ed, eut, lc4, f14, r2, f2, x1, mod4, ln_g, ln_b)


def kernel(x, c, w_ada, b_ada, w_in, w_out, na_rpb, t5_bias, lambda_q1, lambda_k1, lambda_q2,
           lambda_k2, diff_subln, ln1_g, ln1_b, w_query, sub_keys, expert_down, expert_up,
           ln2_g, ln2_b):
    b, seq, d = x.shape
    t = b * seq
    rows = seq // GRID_W
    assert d == D_MODEL and w_ada.shape[0] == DEPTH == 1
    assert rows % NA_QROWS == 0 and rows >= NA_KROWS and seq % DIFF_TQ == 0
    l = 0
    lambda_init = 0.8 - 0.6 * math.exp(-0.3 * l)
    xf = x.reshape(t, d)

    mod4 = _ada(c, w_ada[l], b_ada[l][None, :]).reshape(b, 6, 1, d)
    proj3 = _proj(xf, mod4, w_in[l].astype(BF16), seq).reshape(b, seq, IN_WIDTH)

    nb = _na_bias(na_rpb[l].reshape(-1), rows)
    o_na = _na(proj3, nb, rows).reshape(t, NA_WIDTH)

    dt = _t5_bias(t5_bias.reshape(-1), seq)
    o_df = _diff(proj3, dt, lambda_q1[l][None, :], lambda_k1[l][None, :], lambda_q2[l][None, :],
                 lambda_k2[l][None, :], diff_subln[l][None, :], lambda_init).reshape(t, DIFF_WIDTH)

    x1, h2, st = _mid(o_na, o_df, w_out[l].astype(BF16), xf, mod4, ln1_g[l][None, :],
                      ln1_b[l][None, :], w_query[l].astype(BF16), sub_keys[l].astype(BF16), seq)
    lc, f1, r2, f2 = _topk(st)
    out = _peer(h2, expert_down[l].astype(BF16), expert_up[l].T.astype(BF16), lc, f1, r2, f2,
                x1, mod4, ln2_g[l][None, :], ln2_b[l][None, :], seq)
    return out.reshape(b, seq, d)
```

```python
import functools
import math

import jax
import jax.numpy as jnp
from jax import lax
from jax.experimental import pallas as pl
from jax.experimental.pallas import tpu as pltpu

F32 = jnp.float32
BF16 = jnp.bfloat16

D_MODEL = 1024
DEPTH = 1
GRID_W = 64
NA_HEADS = 8
NA_HEAD_DIM = 64
NA_WIDTH = NA_HEADS * NA_HEAD_DIM
NA_WIN_ROWS = 8
NA_WIN_COLS = 16
DIFF_HEADS = 4
DIFF_HEAD_DIM = 64
DIFF_V_DIM = 2 * DIFF_HEAD_DIM
DIFF_WIDTH = DIFF_HEADS * DIFF_V_DIM
IN_WIDTH = 3 * NA_WIDTH + 3 * DIFF_WIDTH
T5_BUCKETS = 32
T5_MAX_DIST = 128
PEER_HEADS = 8
PEER_NKEYS = 128
PEER_N = PEER_NKEYS * PEER_NKEYS
PEER_TOPK = 16
LN_EPS = 1e-5
DN_ALPHA = (2.0 * DEPTH) ** 0.25
NEG_BIG = -1e30
LANES = 128
BF16_SUBLANES = 16

PROJ_TM = 256
NA_QROWS = 4
NA_KROWS = 12
DIFF_TQ = 256
MID_TM = 256
TOPK_TL = 512
PEER_TM = 512
PEER_TE = 1024
VMEM_LIMIT = 56 * 1024 * 1024

_NT = (((1,), (1,)), ((), ()))


def _cparams(*sem):
    return pltpu.CompilerParams(dimension_semantics=sem, vmem_limit_bytes=VMEM_LIMIT)


def _ln(x):
    mu = jnp.mean(x, axis=-1, keepdims=True)
    xc = x - mu
    var = jnp.mean(xc * xc, axis=-1, keepdims=True)
    return xc * lax.rsqrt(var + LN_EPS)


def _ada_body(c_ref, w_ref, b_ref, o_ref):
    c = c_ref[...]
    s = c * jax.nn.sigmoid(c)
    o_ref[...] = jnp.dot(s, w_ref[...], preferred_element_type=F32) + b_ref[...]


def _ada(c, w, bias):
    b, d = c.shape
    n = w.shape[1]
    return pl.pallas_call(
        _ada_body,
        grid=(n // d,),
        in_specs=[pl.BlockSpec((b, d), lambda j: (0, 0)),
                  pl.BlockSpec((d, d), lambda j: (0, j)),
                  pl.BlockSpec((1, d), lambda j: (0, j))],
        out_specs=pl.BlockSpec((b, d), lambda j: (0, j)),
        out_shape=jax.ShapeDtypeStruct((b, n), F32),
        compiler_params=_cparams("arbitrary"),
        name="ada",
    )(c, w, bias)


def _proj_body(x_ref, sc_ref, sh_ref, w_ref, o_ref):
    h = _ln(x_ref[...]) * (1.0 + sc_ref[...]) + sh_ref[...]
    o_ref[...] = jnp.dot(h.astype(BF16), w_ref[...], preferred_element_type=F32).astype(o_ref.dtype)


def _proj(xf, mod4, w, seq):
    t, d = xf.shape
    n = w.shape[1]
    tpb = seq // PROJ_TM
    return pl.pallas_call(
        _proj_body,
        grid=(t // PROJ_TM,),
        in_specs=[pl.BlockSpec((PROJ_TM, d), lambda i: (i, 0)),
                  pl.BlockSpec((None, None, 1, d), lambda i: (i // tpb, 1, 0, 0)),
                  pl.BlockSpec((None, None, 1, d), lambda i: (i // tpb, 0, 0, 0)),
                  pl.BlockSpec((d, n), lambda i: (0, 0))],
        out_specs=pl.BlockSpec((PROJ_TM, n), lambda i: (i, 0)),
        out_shape=jax.ShapeDtypeStruct((t, n), BF16),
        compiler_params=_cparams("arbitrary"),
        name="proj",
    )(xf, mod4, mod4, w)


def _na_row_offsets(rows):
    kr = min(NA_WIN_ROWS, rows)
    nrb = rows // NA_QROWS
    pats = []
    for rb in range(nrb):
        r0 = rb * NA_QROWS
        s = min(max(r0 - NA_WIN_ROWS // 2, 0), rows - NA_KROWS)
        pat = []
        for qr in range(NA_QROWS):
            r = r0 + qr
            rs = min(max(r - kr // 2, 0), rows - kr)
            pat.append(tuple((s + i) - r + (NA_WIN_ROWS - 1) if rs <= s + i < rs + kr else 15
                             for i in range(NA_KROWS)))
        pats.append(tuple(pat))
    variants = [pats[0], pats[1], pats[-1]]
    assert all(p == variants[1] for p in pats[1:-1])
    return variants


def _na_bias_body(rpb_ref, o_ref, tile_ref, *, variants):
    h = pl.program_id(0)
    nro = 2 * NA_WIN_ROWS - 1
    nco = 2 * NA_WIN_COLS - 1
    q = lax.broadcasted_iota(jnp.int32, (GRID_W, LANES), 0)
    kk = lax.broadcasted_iota(jnp.int32, (GRID_W, LANES), 1)
    k = jnp.bitwise_and(kk, GRID_W - 1)
    col_start = jnp.clip(q - NA_WIN_COLS // 2, 0, GRID_W - NA_WIN_COLS)
    valid = jnp.logical_and(k >= col_start, k < col_start + NA_WIN_COLS)
    co = jnp.clip(k - q, -(NA_WIN_COLS - 1), NA_WIN_COLS - 1) + (NA_WIN_COLS - 1)
    neg = jnp.full((GRID_W, LANES), NEG_BIG, F32)
    for ro in range(nro):
        acc = neg
        for cc in range(nco):
            acc = jnp.where(co == cc, rpb_ref[(h * nro + ro) * nco + cc], acc)
        tile_ref[ro] = jnp.where(valid, acc, NEG_BIG)
    tile_ref[nro] = neg
    left = kk < GRID_W
    for v, pat in enumerate(variants):
        for qr in range(NA_QROWS):
            for kp in range(NA_KROWS // 2):
                a, b = pat[qr][2 * kp], pat[qr][2 * kp + 1]
                o_ref[v, qr * GRID_W:(qr + 1) * GRID_W, kp * LANES:(kp + 1) * LANES] = (
                    jnp.where(left, tile_ref[a], tile_ref[b]))


def _na_bias(rpb_flat, rows):
    variants = _na_row_offsets(rows)
    nq, nk = NA_QROWS * GRID_W, NA_KROWS * GRID_W
    return pl.pallas_call(
        functools.partial(_na_bias_body, variants=variants),
        grid=(NA_HEADS,),
        in_specs=[pl.BlockSpec(memory_space=pltpu.SMEM)],
        out_specs=pl.BlockSpec((3, None, nq, nk), lambda h: (0, h, 0, 0)),
        out_shape=jax.ShapeDtypeStruct((3, NA_HEADS, nq, nk), F32),
        scratch_shapes=[pltpu.VMEM((2 * NA_WIN_ROWS, GRID_W, LANES), F32)],
        compiler_params=_cparams("arbitrary"),
        name="na_bias",
    )(rpb_flat)


def _half_masks(value=1.0):
    lane = lax.broadcasted_iota(jnp.int32, (1, LANES), 1)
    lo = jnp.where(lane < LANES // 2, value, 0.0).astype(BF16)
    hi = jnp.where(lane < LANES // 2, 0.0, value).astype(BF16)
    return lo, hi


def _na_body(q_ref, k_ref, v_ref, b_ref, o_ref, *, rows):
    rb = pl.program_id(2)
    s_row = jnp.clip(rb * NA_QROWS - NA_WIN_ROWS // 2, 0, rows - NA_KROWS)
    start = pl.multiple_of(s_row * GRID_W, GRID_W)
    nk = NA_KROWS * GRID_W
    q = q_ref[...]
    kw = k_ref[pl.ds(start, nk), :]
    vw = v_ref[pl.ds(start, nk), :]
    acc = jnp.zeros(q.shape, F32)
    for h, (mq, mv) in enumerate(zip(_half_masks(NA_HEAD_DIM ** -0.5), _half_masks())):
        s = lax.dot_general(q * mq, kw, _NT, preferred_element_type=F32) + b_ref[h]
        p = jnp.exp(s - jnp.max(s, axis=-1, keepdims=True))
        l = jnp.sum(p, axis=-1, keepdims=True)
        o = jnp.dot(p.astype(BF16), vw * mv, preferred_element_type=F32)
        acc = acc + o * (1.0 / l)
    o_ref[...] = acc.astype(o_ref.dtype)


def _na(proj3, nb, rows):
    b, seq, _ = proj3.shape
    nrb = rows // NA_QROWS
    nq, nk = NA_QROWS * GRID_W, NA_KROWS * GRID_W
    npair = NA_HEADS // 2

    def variant(rb):
        return jnp.where(rb == 0, 0, jnp.where(rb == nrb - 1, 2, 1))

    return pl.pallas_call(
        functools.partial(_na_body, rows=rows),
        grid=(b, npair, nrb),
        in_specs=[pl.BlockSpec((None, nq, LANES), lambda bi, hp, rb: (bi, rb, hp)),
                  pl.BlockSpec((None, seq, LANES), lambda bi, hp, rb: (bi, 0, npair + hp)),
                  pl.BlockSpec((None, seq, LANES), lambda bi, hp, rb: (bi, 0, 2 * npair + hp)),
                  pl.BlockSpec((None, 2, nq, nk), lambda bi, hp, rb: (variant(rb), hp, 0, 0))],
        out_specs=pl.BlockSpec((None, nq, LANES), lambda bi, hp, rb: (bi, rb, hp)),
        out_shape=jax.ShapeDtypeStruct((b, seq, NA_WIDTH), BF16),
        compiler_params=_cparams("arbitrary", "arbitrary", "arbitrary"),
        name="na",
    )(proj3, proj3, proj3, nb)


def _t5_bucket_of(rel):
    nb = T5_BUCKETS // 2
    max_exact = nb // 2
    bucket = jnp.where(rel > 0, nb, 0)
    n = jnp.abs(rel)
    nf = jnp.maximum(n, 1).astype(jnp.float32)
    large = max_exact + (jnp.log(nf / max_exact) / math.log(T5_MAX_DIST / max_exact)
                         * (nb - max_exact)).astype(jnp.int32)
    large = jnp.minimum(large, nb - 1)
    return bucket + jnp.where(n < max_exact, n, large)


def _t5_bias_body(bkt_ref, t5_ref, o_ref):
    h = pl.program_id(0)
    bkt = bkt_ref[...]
    acc = jnp.zeros(bkt.shape, F32)
    for bb in range(T5_BUCKETS):
        acc = jnp.where(bkt == bb, t5_ref[bb * DIFF_HEADS + h], acc)
    o_ref[...] = acc


def _t5_bias(t5_flat, seq):
    nq = seq // DIFF_TQ
    nt = 2 * nq - 1
    d = jnp.arange(nt, dtype=jnp.int32)[:, None, None] - (nq - 1)
    qq = jnp.arange(DIFF_TQ, dtype=jnp.int32)[None, :, None]
    kk = jnp.arange(DIFF_TQ, dtype=jnp.int32)[None, None, :]
    bkt = _t5_bucket_of(d * DIFF_TQ + kk - qq)
    return pl.pallas_call(
        _t5_bias_body,
        grid=(DIFF_HEADS, nt),
        in_specs=[pl.BlockSpec((None, DIFF_TQ, DIFF_TQ), lambda h, t: (t, 0, 0)),
                  pl.BlockSpec(memory_space=pltpu.SMEM)],
        out_specs=pl.BlockSpec((None, None, DIFF_TQ, DIFF_TQ), lambda h, t: (h, t, 0, 0)),
        out_shape=jax.ShapeDtypeStruct((DIFF_HEADS, nt, DIFF_TQ, DIFF_TQ), F32),
        compiler_params=_cparams("arbitrary", "arbitrary"),
        name="t5_bias",
    )(bkt, t5_flat)


def _diff_body(q_ref, k_ref, v_ref, dt_ref, lq1_ref, lk1_ref, lq2_ref, lk2_ref, g_ref, o_ref,
               *, nq, lambda_init):
    qi = pl.program_id(2)
    q = q_ref[...]
    lam = (jnp.exp(jnp.sum(lq1_ref[...] * lk1_ref[...], axis=-1, keepdims=True))
           - jnp.exp(jnp.sum(lq2_ref[...] * lk2_ref[...], axis=-1, keepdims=True)) + lambda_init)
    v = v_ref[...]
    outs = []
    for m in _half_masks(DIFF_HEAD_DIM ** -0.5):
        qm = q * m
        pieces = []
        for kb in range(nq):
            s = lax.dot_general(qm, k_ref[kb * DIFF_TQ:(kb + 1) * DIFF_TQ, :], _NT,
                                preferred_element_type=F32)
            pieces.append(s + dt_ref[nq - 1 - qi + kb])
        s = jnp.concatenate(pieces, axis=1)
        p = jnp.exp(s - jnp.max(s, axis=-1, keepdims=True))
        l = jnp.sum(p, axis=-1, keepdims=True)
        outs.append((jnp.dot(p.astype(BF16), v, preferred_element_type=F32), l))
    (o1, l1), (o2, l2) = outs
    o = o1 * (1.0 / l1) - o2 * (lam / l2)
    o = o * lax.rsqrt(jnp.mean(o * o, axis=-1, keepdims=True) + LN_EPS) * g_ref[...]
    o_ref[...] = (o * (1.0 - lambda_init)).astype(o_ref.dtype)


def _diff(proj3, dt, lq1, lk1, lq2, lk2, subln, lambda_init):
    b, seq, _ = proj3.shape
    nq = seq // DIFF_TQ
    nt = 2 * nq - 1
    c0 = 3 * NA_WIDTH // LANES
    vec = lambda n: pl.BlockSpec((1, n), lambda h, bi, qi: (0, 0))
    return pl.pallas_call(
        functools.partial(_diff_body, nq=nq, lambda_init=lambda_init),
        grid=(DIFF_HEADS, b, nq),
        in_specs=[pl.BlockSpec((None, DIFF_TQ, LANES), lambda h, bi, qi: (bi, qi, c0 + h)),
                  pl.BlockSpec((None, seq, LANES), lambda h, bi, qi: (bi, 0, c0 + DIFF_HEADS + h)),
                  pl.BlockSpec((None, seq, LANES), lambda h, bi, qi: (bi, 0, c0 + 2 * DIFF_HEADS + h)),
                  pl.BlockSpec((None, nt, DIFF_TQ, DIFF_TQ), lambda h, bi, qi: (h, 0, 0, 0)),
                  vec(DIFF_HEAD_DIM), vec(DIFF_HEAD_DIM), vec(DIFF_HEAD_DIM), vec(DIFF_HEAD_DIM),
                  vec(DIFF_V_DIM)],
        out_specs=pl.BlockSpec((None, DIFF_TQ, LANES), lambda h, bi, qi: (bi, qi, h)),
        out_shape=jax.ShapeDtypeStruct((b, seq, DIFF_WIDTH), BF16),
        compiler_params=_cparams("arbitrary", "arbitrary", "arbitrary"),
        name="diff",
    )(proj3, proj3, proj3, dt, lq1, lk1, lq2, lk2, subln)


def _mid_body(ona_ref, odf_ref, wout_ref, x_ref, ga_ref, scf_ref, shf_ref, g1_ref, b1_ref,
              wq_ref, sk_ref, x1_ref, h2_ref, st_ref):
    mix = (jnp.dot(ona_ref[...], wout_ref[0:NA_WIDTH, :], preferred_element_type=F32)
           + jnp.dot(odf_ref[...], wout_ref[NA_WIDTH:, :], preferred_element_type=F32))
    y = DN_ALPHA * x_ref[...] + (1.0 + ga_ref[...]) * mix
    x1 = _ln(y) * g1_ref[...] + b1_ref[...]
    x1_ref[...] = x1
    h2 = (_ln(x1) * (1.0 + scf_ref[...]) + shf_ref[...]).astype(BF16)
    h2_ref[...] = h2
    qp = jnp.dot(h2, wq_ref[...], preferred_element_type=F32).astype(BF16)
    for hp in range(2 * PEER_HEADS):
        st_ref[hp] = lax.dot_general(sk_ref[hp % 2], qp[:, hp * PEER_NKEYS:(hp + 1) * PEER_NKEYS],
                                     _NT, preferred_element_type=F32)


def _mid(o_na, o_df, w_out, xf, mod4, ln_g, ln_b, w_query, sub_keys, seq):
    t, d = xf.shape
    tpb = seq // MID_TM
    nqk = w_query.shape[1]
    modspec = lambda j: pl.BlockSpec((None, None, 1, d), lambda i: (i // tpb, j, 0, 0))
    full = lambda *s: pl.BlockSpec(s, lambda i: (0,) * len(s))
    return pl.pallas_call(
        _mid_body,
        grid=(t // MID_TM,),
        in_specs=[pl.BlockSpec((MID_TM, NA_WIDTH), lambda i: (i, 0)),
                  pl.BlockSpec((MID_TM, DIFF_WIDTH), lambda i: (i, 0)),
                  full(d, d),
                  pl.BlockSpec((MID_TM, d), lambda i: (i, 0)),
                  modspec(2), modspec(4), modspec(3),
                  full(1, d), full(1, d),
                  full(d, nqk),
                  full(2, PEER_NKEYS, PEER_NKEYS)],
        out_specs=[pl.BlockSpec((MID_TM, d), lambda i: (i, 0)),
                   pl.BlockSpec((MID_TM, d), lambda i: (i, 0)),
                   pl.BlockSpec((2 * PEER_HEADS, PEER_NKEYS, MID_TM), lambda i: (0, 0, i))],
        out_shape=[jax.ShapeDtypeStruct((t, d), F32),
                   jax.ShapeDtypeStruct((t, d), BF16),
                   jax.ShapeDtypeStruct((2 * PEER_HEADS, PEER_NKEYS, t), F32)],
        compiler_params=_cparams("arbitrary"),
        name="mid",
    )(o_na, o_df, w_out, xf, mod4, mod4, mod4, ln_g, ln_b, w_query, sub_keys)


def _extract_top(work, order, tie_rule, out_ref=None, tok=None):
    rank = jnp.full(work.shape, float(PEER_TOPK), F32)
    big = float(PEER_N)
    for r in range(PEER_TOPK):
        m = jnp.max(work, axis=0, keepdims=True)
        sel = work == m
        if tie_rule:
            first = jnp.min(jnp.where(sel, order, big), axis=0, keepdims=True)
            sel = order == first
        rank = jnp.where(sel, float(r), rank)
        work = jnp.where(sel, -jnp.inf, work)
        if out_ref is not None:
            out_ref[r:r + 1, tok] = m
    return rank


def _pair_cells(op, u1, u2):
    g = [op(u1[0:1], u2[0:8]), op(u1[0:1], u2[8:16])]
    for a in range(1, 8):
        g.append(op(u1[a:a + 1], u2[0:8]))
    g.append(op(u1[8:16], u2[0:1]))
    return jnp.concatenate(g, axis=0)


def _topk_body(st_ref, lc_ref, f1_ref, r2_ref, f2_ref, t_ref, rk_ref, l_ref):
    tl = st_ref.shape[-1]
    chunks = [slice(c * LANES, (c + 1) * LANES) for c in range(tl // LANES)]
    key_id = lax.broadcasted_iota(jnp.int32, (PEER_NKEYS, LANES), 0).astype(F32)

    def half_ranks(tie_rule):
        for tok in chunks:
            for p in range(2):
                rk_ref[p, :, tok] = _extract_top(st_ref[p, :, tok], key_id, tie_rule,
                                                 t_ref.at[p], tok)

    half_ranks(False)
    taken = jnp.where(rk_ref[...] < float(PEER_TOPK), 1.0, 0.0)
    most = jnp.max(jnp.sum(taken, axis=1, keepdims=True))

    @pl.when(most > float(PEER_TOPK))
    def _():
        half_ranks(True)

    row = lax.broadcasted_iota(jnp.int32, (80, LANES), 0)
    mid = row - 16
    cell_id = jnp.where(row < 16, row,
                        jnp.where(row < 72,
                                  (jnp.right_shift(mid, 3) + 1) * 16 + jnp.bitwise_and(mid, 7),
                                  (row - 64) * 16)).astype(F32)
    for tok in chunks:
        s1 = st_ref[0, :, tok]
        s2 = st_ref[1, :, tok]
        rank1 = rk_ref[0, :, tok]
        rank2 = rk_ref[1, :, tok]
        t1 = t_ref[0, :, tok]
        t2 = t_ref[1, :, tok]
        crank = _extract_top(_pair_cells(jnp.add, t1, t2), cell_id, True)
        chosen = jnp.where(crank < float(PEER_TOPK), 1.0, 0.0)
        e1 = jnp.exp(t1 - t1[0:1])
        e2 = jnp.exp(t2 - t2[0:1])
        z = jnp.sum(chosen * _pair_cells(jnp.multiply, e1, e2), axis=0, keepdims=True)
        inv_z = 1.0 / z
        l_ref[0:1, tok] = jnp.sum(chosen[0:16], axis=0, keepdims=True)
        for a in range(1, 8):
            l_ref[a:a + 1, tok] = jnp.sum(chosen[8 + 8 * a:16 + 8 * a], axis=0, keepdims=True)
        l_ref[8:16, tok] = chosen[72:80]
        lcnt = l_ref[:, tok]
        lc = jnp.zeros((PEER_NKEYS, LANES), F32)
        for a in range(PEER_TOPK):
            lc = jnp.where(rank1 == float(a), lcnt[a:a + 1], lc)
        lc_ref[:, tok] = lc
        f1_ref[:, tok] = jnp.exp(s1 - t1[0:1]) * inv_z
        r2_ref[:, tok] = rank2.astype(r2_ref.dtype)
        f2_ref[:, tok] = jnp.where(rank2 < float(PEER_TOPK), jnp.exp(s2 - t2[0:1]),
                                   0.0).astype(f2_ref.dtype)


def _topk(st):
    _, nk, t = st.shape
    tl = min(TOPK_TL, t)
    spec = pl.BlockSpec((None, nk, tl), lambda i, h: (h, 0, i))
    rows = jax.ShapeDtypeStruct((PEER_HEADS, nk, t), F32)
    cols = jax.ShapeDtypeStruct((PEER_HEADS, nk, t), BF16)
    return pl.pallas_call(
        _topk_body,
        grid=(t // tl, PEER_HEADS),
        in_specs=[pl.BlockSpec((2, nk, tl), lambda i, h: (h, 0, i))],
        out_specs=[spec, spec, spec, spec],
        out_shape=[rows, rows, cols, cols],
        scratch_shapes=[pltpu.VMEM((2, PEER_TOPK, tl), F32), pltpu.VMEM((2, nk, tl), F32),
                        pltpu.VMEM((PEER_TOPK, tl), F32)],
        compiler_params=_cparams("arbitrary", "arbitrary"),
        name="topk",
    )(st)


def _peer_body(h2_ref, ed_ref, eut_ref, lc_ref, f1_ref, r2_ref, f2_ref, x1_ref, gf_ref,
               g2_ref, b2_ref, o_ref, acc_ref, wt_ref, at_ref, *, ni):
    et = pl.program_id(1)

    @pl.when(et == 0)
    def _():
        acc_ref[...] = jnp.zeros(acc_ref.shape, F32)

    tm = h2_ref.shape[0]
    at_ref[...] = lax.dot_general(ed_ref[...], h2_ref[...], _NT, preferred_element_type=F32)
    sub = BF16_SUBLANES
    for c in range(tm // LANES):
        tok = slice(c * LANES, (c + 1) * LANES)
        for ii in range(ni):
            g = [None] * (PEER_NKEYS // sub)
            for h in range(PEER_HEADS):
                cnt = jnp.broadcast_to(lc_ref[h, ii:ii + 1, tok], (sub, LANES)).astype(BF16)
                fac = jnp.broadcast_to(f1_ref[h, ii:ii + 1, tok], (sub, LANES)).astype(BF16)
                for k in range(len(g)):
                    rk = slice(k * sub, (k + 1) * sub)
                    term = jnp.where(r2_ref[h, rk, tok] < cnt, f2_ref[h, rk, tok], 0) * fac
                    g[k] = term if g[k] is None else g[k] + term
            for k in range(len(g)):
                rows = slice(ii * PEER_NKEYS + k * sub, ii * PEER_NKEYS + (k + 1) * sub)
                a = at_ref[rows, tok]
                act = 0.5 * a * (1.0 + lax.erf(a * (2.0 ** -0.5)))
                wt_ref[rows, tok] = g[k] * act.astype(BF16)
    acc_ref[...] += jnp.dot(eut_ref[...], wt_ref[...], preferred_element_type=F32)

    @pl.when(et == pl.num_programs(1) - 1)
    def _():
        f = acc_ref[...].T
        y = DN_ALPHA * x1_ref[...] + (1.0 + gf_ref[...]) * f
        o_ref[...] = _ln(y) * g2_ref[...] + b2_ref[...]


def _peer(h2, ed, eut, lc, f1, r2, f2, x1, mod4, ln_g, ln_b, seq):
    t, d = h2.shape
    tm = min(PEER_TM, t)
    te = PEER_TE
    ni = te // PEER_NKEYS
    ng = PEER_NKEYS // ni
    tpb = seq // tm
    lc4 = lc.reshape(PEER_HEADS, ng, ni, t)
    f14 = f1.reshape(PEER_HEADS, ng, ni, t)
    rowspec = pl.BlockSpec((PEER_HEADS, None, ni, tm), lambda i, e: (0, e, 0, i))
    colspec = pl.BlockSpec((PEER_HEADS, PEER_NKEYS, tm), lambda i, e: (0, 0, i))
    full = lambda *s: pl.BlockSpec(s, lambda i, e: (0,) * len(s))
    return pl.pallas_call(
        functools.partial(_peer_body, ni=ni),
        grid=(t // tm, PEER_N // te),
        in_specs=[pl.BlockSpec((tm, d), lambda i, e: (i, 0)),
                  pl.BlockSpec((te, d), lambda i, e: (e, 0)),
                  pl.BlockSpec((d, te), lambda i, e: (0, e)),
                  rowspec, rowspec, colspec, colspec,
                  pl.BlockSpec((tm, d), lambda i, e: (i, 0)),
                  pl.BlockSpec((None, None, 1, d), lambda i, e: (i // tpb, 5, 0, 0)),
                  full(1, d), full(1, d)],
        out_specs=pl.BlockSpec((tm, d), lambda i, e: (i, 0)),
        out_shape=jax.ShapeDtypeStruct((t, d), F32),
        scratch_shapes=[pltpu.VMEM((d, tm), F32), pltpu.VMEM((te, tm), BF16),
                        pltpu.VMEM((te, tm), F32)],
        compiler_params=_cparams("arbitrary", "arbitrary"),
        name="peer",
    )(h2, ed, eut, lc4, f14, r2, f2, x1, mod4, ln_g, ln_b)


def kernel(x, c, w_ada, b_ada, w_in, w_out, na_rpb, t5_bias, lambda_q1, lambda_k1, lambda_q2,
           lambda_k2, diff_subln, ln1_g, ln1_b, w_query, sub_keys, expert_down, expert_up,
           ln2_g, ln2_b):
    b, seq, d = x.shape
    t = b * seq
    rows = seq // GRID_W
    assert d == D_MODEL and w_ada.shape[0] == DEPTH == 1
    assert rows % NA_QROWS == 0 and rows >= NA_KROWS and seq % DIFF_TQ == 0
    l = 0
    lambda_init = 0.8 - 0.6 * math.exp(-0.3 * l)
    xf = x.reshape(t, d)

    mod4 = _ada(c, w_ada[l], b_ada[l][None, :]).reshape(b, 6, 1, d)
    proj3 = _proj(xf, mod4, w_in[l].astype(BF16), seq).reshape(b, seq, IN_WIDTH)

    nb = _na_bias(na_rpb[l].reshape(-1), rows)
    o_na = _na(proj3, nb, rows).reshape(t, NA_WIDTH)

    dt = _t5_bias(t5_bias.reshape(-1), seq)
    o_df = _diff(proj3, dt, lambda_q1[l][None, :], lambda_k1[l][None, :], lambda_q2[l][None, :],
                 lambda_k2[l][None, :], diff_subln[l][None, :], lambda_init).reshape(t, DIFF_WIDTH)

    x1, h2, st = _mid(o_na, o_df, w_out[l].astype(BF16), xf, mod4, ln1_g[l][None, :],
                      ln1_b[l][None, :], w_query[l].astype(BF16), sub_keys[l].astype(BF16), seq)
    lc, f1, r2, f2 = _topk(st)
    out = _peer(h2, expert_down[l].astype(BF16), expert_up[l].T.astype(BF16), lc, f1, r2, f2,
                x1, mod4, ln2_g[l][None, :], ln2_b[l][None, :], seq)
    return out.reshape(b, seq, d)
```

```python
import functools
import math

import jax
import jax.numpy as jnp
from jax import lax
from jax.experimental import pallas as pl
from jax.experimental.pallas import tpu as pltpu

F32 = jnp.float32
BF16 = jnp.bfloat16

D_MODEL = 1024
DEPTH = 1
GRID_W = 64
NA_HEADS = 8
NA_HEAD_DIM = 64
NA_WIDTH = NA_HEADS * NA_HEAD_DIM
NA_WIN_ROWS = 8
NA_WIN_COLS = 16
DIFF_HEADS = 4
DIFF_HEAD_DIM = 64
DIFF_V_DIM = 2 * DIFF_HEAD_DIM
DIFF_WIDTH = DIFF_HEADS * DIFF_V_DIM
IN_WIDTH = 3 * NA_WIDTH + 3 * DIFF_WIDTH
T5_BUCKETS = 32
T5_MAX_DIST = 128
PEER_HEADS = 8
PEER_NKEYS = 128
PEER_N = PEER_NKEYS * PEER_NKEYS
PEER_TOPK = 16
LN_EPS = 1e-5
DN_ALPHA = (2.0 * DEPTH) ** 0.25
NEG_BIG = -1e30
LANES = 128
BF16_SUBLANES = 16

PROJ_TM = 256
NA_QROWS = 4
NA_KROWS = 12
DIFF_TQ = 256
MID_TM = 256
TOPK_TL = 512
PEER_TM = 512
PEER_TE = 1024
VMEM_LIMIT = 56 * 1024 * 1024

_NT = (((1,), (1,)), ((), ()))


def _cparams(*sem):
    return pltpu.CompilerParams(dimension_semantics=sem, vmem_limit_bytes=VMEM_LIMIT)


def _ln(x):
    mu = jnp.mean(x, axis=-1, keepdims=True)
    xc = x - mu
    var = jnp.mean(xc * xc, axis=-1, keepdims=True)
    return xc * lax.rsqrt(var + LN_EPS)


def _ada_body(c_ref, w_ref, b_ref, o_ref):
    c = c_ref[...]
    s = c * jax.nn.sigmoid(c)
    o_ref[...] = jnp.dot(s, w_ref[...], preferred_element_type=F32) + b_ref[...]


def _ada(c, w, bias):
    b, d = c.shape
    n = w.shape[1]
    return pl.pallas_call(
        _ada_body,
        grid=(n // d,),
        in_specs=[pl.BlockSpec((b, d), lambda j: (0, 0)),
                  pl.BlockSpec((d, d), lambda j: (0, j)),
                  pl.BlockSpec((1, d), lambda j: (0, j))],
        out_specs=pl.BlockSpec((b, d), lambda j: (0, j)),
        out_shape=jax.ShapeDtypeStruct((b, n), F32),
        compiler_params=_cparams("arbitrary"),
        name="ada",
    )(c, w, bias)


def _proj_body(x_ref, sc_ref, sh_ref, w_ref, o_ref):
    h = _ln(x_ref[...]) * (1.0 + sc_ref[...]) + sh_ref[...]
    o_ref[...] = jnp.dot(h.astype(BF16), w_ref[...], preferred_element_type=F32).astype(o_ref.dtype)


def _proj(xf, mod4, w, seq):
    t, d = xf.shape
    n = w.shape[1]
    tpb = seq // PROJ_TM
    return pl.pallas_call(
        _proj_body,
        grid=(t // PROJ_TM,),
        in_specs=[pl.BlockSpec((PROJ_TM, d), lambda i: (i, 0)),
                  pl.BlockSpec((None, None, 1, d), lambda i: (i // tpb, 1, 0, 0)),
                  pl.BlockSpec((None, None, 1, d), lambda i: (i // tpb, 0, 0, 0)),
                  pl.BlockSpec((d, n), lambda i: (0, 0))],
        out_specs=pl.BlockSpec((PROJ_TM, n), lambda i: (i, 0)),
        out_shape=jax.ShapeDtypeStruct((t, n), BF16),
        compiler_params=_cparams("arbitrary"),
        name="proj",
    )(xf, mod4, mod4, w)


def _na_row_offsets(rows):
    kr = min(NA_WIN_ROWS, rows)
    nrb = rows // NA_QROWS
    pats = []
    for rb in range(nrb):
        r0 = rb * NA_QROWS
        s = min(max(r0 - NA_WIN_ROWS // 2, 0), rows - NA_KROWS)
        pat = []
        for qr in range(NA_QROWS):
            r = r0 + qr
            rs = min(max(r - kr // 2, 0), rows - kr)
            pat.append(tuple((s + i) - r + (NA_WIN_ROWS - 1) if rs <= s + i < rs + kr else 15
                             for i in range(NA_KROWS)))
        pats.append(tuple(pat))
    variants = [pats[0], pats[1], pats[-1]]
    assert all(p == variants[1] for p in pats[1:-1])
    return variants


def _na_bias_body(rpb_ref, o_ref, tile_ref, *, variants):
    h = pl.program_id(0)
    nro = 2 * NA_WIN_ROWS - 1
    nco = 2 * NA_WIN_COLS - 1
    q = lax.broadcasted_iota(jnp.int32, (GRID_W, LANES), 0)
    kk = lax.broadcasted_iota(jnp.int32, (GRID_W, LANES), 1)
    k = jnp.bitwise_and(kk, GRID_W - 1)
    col_start = jnp.clip(q - NA_WIN_COLS // 2, 0, GRID_W - NA_WIN_COLS)
    valid = jnp.logical_and(k >= col_start, k < col_start + NA_WIN_COLS)
    co = jnp.clip(k - q, -(NA_WIN_COLS - 1), NA_WIN_COLS - 1) + (NA_WIN_COLS - 1)
    neg = jnp.full((GRID_W, LANES), NEG_BIG, F32)
    for ro in range(nro):
        acc = neg
        for cc in range(nco):
            acc = jnp.where(co == cc, rpb_ref[(h * nro + ro) * nco + cc], acc)
        tile_ref[ro] = jnp.where(valid, acc, NEG_BIG)
    tile_ref[nro] = neg
    left = kk < GRID_W
    for v, pat in enumerate(variants):
        for qr in range(NA_QROWS):
            for kp in range(NA_KROWS // 2):
                a, b = pat[qr][2 * kp], pat[qr][2 * kp + 1]
                o_ref[v, qr * GRID_W:(qr + 1) * GRID_W, kp * LANES:(kp + 1) * LANES] = (
                    jnp.where(left, tile_ref[a], tile_ref[b]))


def _na_bias(rpb_flat, rows):
    variants = _na_row_offsets(rows)
    nq, nk = NA_QROWS * GRID_W, NA_KROWS * GRID_W
    return pl.pallas_call(
        functools.partial(_na_bias_body, variants=variants),
        grid=(NA_HEADS,),
        in_specs=[pl.BlockSpec(memory_space=pltpu.SMEM)],
        out_specs=pl.BlockSpec((3, None, nq, nk), lambda h: (0, h, 0, 0)),
        out_shape=jax.ShapeDtypeStruct((3, NA_HEADS, nq, nk), F32),
        scratch_shapes=[pltpu.VMEM((2 * NA_WIN_ROWS, GRID_W, LANES), F32)],
        compiler_params=_cparams("arbitrary"),
        name="na_bias",
    )(rpb_flat)


def _half_masks(value=1.0):
    lane = lax.broadcasted_iota(jnp.int32, (1, LANES), 1)
    lo = jnp.where(lane < LANES // 2, value, 0.0).astype(BF16)
    hi = jnp.where(lane < LANES // 2, 0.0, value).astype(BF16)
    return lo, hi


def _na_body(q_ref, k_ref, v_ref, b_ref, o_ref, *, rows):
    rb = pl.program_id(2)
    s_row = jnp.clip(rb * NA_QROWS - NA_WIN_ROWS // 2, 0, rows - NA_KROWS)
    start = pl.multiple_of(s_row * GRID_W, GRID_W)
    nk = NA_KROWS * GRID_W
    q = q_ref[...]
    kw = k_ref[pl.ds(start, nk), :]
    vw = v_ref[pl.ds(start, nk), :]
    acc = jnp.zeros(q.shape, F32)
    for h, (mq, mv) in enumerate(zip(_half_masks(NA_HEAD_DIM ** -0.5), _half_masks())):
        s = lax.dot_general(q * mq, kw, _NT, preferred_element_type=F32) + b_ref[h]
        p = jnp.exp(s - jnp.max(s, axis=-1, keepdims=True))
        l = jnp.sum(p, axis=-1, keepdims=True)
        o = jnp.dot(p.astype(BF16), vw * mv, preferred_element_type=F32)
        acc = acc + o * (1.0 / l)
    o_ref[...] = acc.astype(o_ref.dtype)


def _na(proj3, nb, rows):
    b, seq, _ = proj3.shape
    nrb = rows // NA_QROWS
    nq, nk = NA_QROWS * GRID_W, NA_KROWS * GRID_W
    npair = NA_HEADS // 2

    def variant(rb):
        return jnp.where(rb == 0, 0, jnp.where(rb == nrb - 1, 2, 1))

    return pl.pallas_call(
        functools.partial(_na_body, rows=rows),
        grid=(b, npair, nrb),
        in_specs=[pl.BlockSpec((None, nq, LANES), lambda bi, hp, rb: (bi, rb, hp)),
                  pl.BlockSpec((None, seq, LANES), lambda bi, hp, rb: (bi, 0, npair + hp)),
                  pl.BlockSpec((None, seq, LANES), lambda bi, hp, rb: (bi, 0, 2 * npair + hp)),
                  pl.BlockSpec((None, 2, nq, nk), lambda bi, hp, rb: (variant(rb), hp, 0, 0))],
        out_specs=pl.BlockSpec((None, nq, LANES), lambda bi, hp, rb: (bi, rb, hp)),
        out_shape=jax.ShapeDtypeStruct((b, seq, NA_WIDTH), BF16),
        compiler_params=_cparams("arbitrary", "arbitrary", "arbitrary"),
        name="na",
    )(proj3, proj3, proj3, nb)


def _t5_thresholds():
    nb = T5_BUCKETS // 2
    me = nb // 2
    span = nb - me
    ratio = T5_MAX_DIST // me
    out = []
    for m in range(1, span):
        n = me
        while n ** span < me ** span * ratio ** m:
            n += 1
        out.append(n)
    return out


def _t5_bias_body(t5_ref, o_ref, *, nq):
    h = pl.program_id(0)
    nb = T5_BUCKETS // 2
    me = nb // 2
    q = lax.broadcasted_iota(jnp.int32, (DIFF_TQ, DIFF_TQ), 0)
    k = lax.broadcasted_iota(jnp.int32, (DIFF_TQ, DIFF_TQ), 1)
    rel = (pl.program_id(1) - (nq - 1)) * DIFF_TQ + k - q
    n = jnp.abs(rel)
    large = jnp.full(n.shape, me, jnp.int32)
    for thr in _t5_thresholds():
        large = large + jnp.where(n >= thr, 1, 0)
    bkt = jnp.where(rel > 0, nb, 0) + jnp.where(n < me, n, large)
    acc = jnp.zeros(bkt.shape, F32)
    for bb in range(T5_BUCKETS):
        acc = jnp.where(bkt == bb, t5_ref[bb * DIFF_HEADS + h], acc)
    o_ref[...] = acc


def _t5_bias(t5_flat, seq):
    nq = seq // DIFF_TQ
    nt = 2 * nq - 1
    return pl.pallas_call(
        functools.partial(_t5_bias_body, nq=nq),
        grid=(DIFF_HEADS, nt),
        in_specs=[pl.BlockSpec(memory_space=pltpu.SMEM)],
        out_specs=pl.BlockSpec((None, None, DIFF_TQ, DIFF_TQ), lambda h, t: (h, t, 0, 0)),
        out_shape=jax.ShapeDtypeStruct((DIFF_HEADS, nt, DIFF_TQ, DIFF_TQ), F32),
        compiler_params=_cparams("arbitrary", "arbitrary"),
        name="t5_bias",
    )(t5_flat)


def _diff_body(q_ref, k_ref, v_ref, dt_ref, lq1_ref, lk1_ref, lq2_ref, lk2_ref, g_ref, o_ref,
               *, nq, lambda_init):
    qi = pl.program_id(2)
    q = q_ref[...]
    lam = (jnp.exp(jnp.sum(lq1_ref[...] * lk1_ref[...], axis=-1, keepdims=True))
           - jnp.exp(jnp.sum(lq2_ref[...] * lk2_ref[...], axis=-1, keepdims=True)) + lambda_init)
    maps = []
    for m in _half_masks(DIFF_HEAD_DIM ** -0.5):
        qm = q * m
        pieces = []
        for kb in range(nq):
            s = lax.dot_general(qm, k_ref[kb * DIFF_TQ:(kb + 1) * DIFF_TQ, :], _NT,
                                preferred_element_type=F32)
            pieces.append(s + dt_ref[nq - 1 - qi + kb])
        s = jnp.concatenate(pieces, axis=1)
        p = jnp.exp(s - jnp.max(s, axis=-1, keepdims=True))
        maps.append((p, jnp.sum(p, axis=-1, keepdims=True)))
    (p1, l1), (p2, l2) = maps
    w = p1 * (1.0 / l1) - p2 * (lam / l2)
    o = jnp.dot(w.astype(BF16), v_ref[...], preferred_element_type=F32)
    o = o * lax.rsqrt(jnp.mean(o * o, axis=-1, keepdims=True) + LN_EPS) * g_ref[...]
    o_ref[...] = (o * (1.0 - lambda_init)).astype(o_ref.dtype)


def _diff(proj3, dt, lq1, lk1, lq2, lk2, subln, lambda_init):
    b, seq, _ = proj3.shape
    nq = seq // DIFF_TQ
    nt = 2 * nq - 1
    c0 = 3 * NA_WIDTH // LANES
    vec = lambda n: pl.BlockSpec((1, n), lambda h, bi, qi: (0, 0))
    return pl.pallas_call(
        functools.partial(_diff_body, nq=nq, lambda_init=lambda_init),
        grid=(DIFF_HEADS, b, nq),
        in_specs=[pl.BlockSpec((None, DIFF_TQ, LANES), lambda h, bi, qi: (bi, qi, c0 + h)),
                  pl.BlockSpec((None, seq, LANES), lambda h, bi, qi: (bi, 0, c0 + DIFF_HEADS + h)),
                  pl.BlockSpec((None, seq, LANES), lambda h, bi, qi: (bi, 0, c0 + 2 * DIFF_HEADS + h)),
                  pl.BlockSpec((None, nt, DIFF_TQ, DIFF_TQ), lambda h, bi, qi: (h, 0, 0, 0)),
                  vec(DIFF_HEAD_DIM), vec(DIFF_HEAD_DIM), vec(DIFF_HEAD_DIM), vec(DIFF_HEAD_DIM),
                  vec(DIFF_V_DIM)],
        out_specs=pl.BlockSpec((None, DIFF_TQ, LANES), lambda h, bi, qi: (bi, qi, h)),
        out_shape=jax.ShapeDtypeStruct((b, seq, DIFF_WIDTH), BF16),
        compiler_params=_cparams("arbitrary", "arbitrary", "arbitrary"),
        name="diff",
    )(proj3, proj3, proj3, dt, lq1, lk1, lq2, lk2, subln)


def _mid_body(ona_ref, odf_ref, wout_ref, x_ref, ga_ref, scf_ref, shf_ref, g1_ref, b1_ref,
              wq_ref, sk_ref, x1_ref, h2_ref, st_ref):
    mix = (jnp.dot(ona_ref[...], wout_ref[0:NA_WIDTH, :], preferred_element_type=F32)
           + jnp.dot(odf_ref[...], wout_ref[NA_WIDTH:, :], preferred_element_type=F32))
    y = DN_ALPHA * x_ref[...] + (1.0 + ga_ref[...]) * mix
    x1 = _ln(y) * g1_ref[...] + b1_ref[...]
    x1_ref[...] = x1
    h2 = (_ln(x1) * (1.0 + scf_ref[...]) + shf_ref[...]).astype(BF16)
    h2_ref[...] = h2
    qp = jnp.dot(h2, wq_ref[...], preferred_element_type=F32).astype(BF16)
    for hp in range(2 * PEER_HEADS):
        st_ref[hp] = lax.dot_general(sk_ref[hp % 2], qp[:, hp * PEER_NKEYS:(hp + 1) * PEER_NKEYS],
                                     _NT, preferred_element_type=F32)


def _mid(o_na, o_df, w_out, xf, mod4, ln_g, ln_b, w_query, sub_keys, seq):
    t, d = xf.shape
    tpb = seq // MID_TM
    nqk = w_query.shape[1]
    modspec = lambda j: pl.BlockSpec((None, None, 1, d), lambda i: (i // tpb, j, 0, 0))
    full = lambda *s: pl.BlockSpec(s, lambda i: (0,) * len(s))
    return pl.pallas_call(
        _mid_body,
        grid=(t // MID_TM,),
        in_specs=[pl.BlockSpec((MID_TM, NA_WIDTH), lambda i: (i, 0)),
                  pl.BlockSpec((MID_TM, DIFF_WIDTH), lambda i: (i, 0)),
                  full(d, d),
                  pl.BlockSpec((MID_TM, d), lambda i: (i, 0)),
                  modspec(2), modspec(4), modspec(3),
                  full(1, d), full(1, d),
                  full(d, nqk),
                  full(2, PEER_NKEYS, PEER_NKEYS)],
        out_specs=[pl.BlockSpec((MID_TM, d), lambda i: (i, 0)),
                   pl.BlockSpec((MID_TM, d), lambda i: (i, 0)),
                   pl.BlockSpec((2 * PEER_HEADS, PEER_NKEYS, MID_TM), lambda i: (0, 0, i))],
        out_shape=[jax.ShapeDtypeStruct((t, d), F32),
                   jax.ShapeDtypeStruct((t, d), BF16),
                   jax.ShapeDtypeStruct((2 * PEER_HEADS, PEER_NKEYS, t), F32)],
        compiler_params=_cparams("arbitrary"),
        name="mid",
    )(o_na, o_df, w_out, xf, mod4, mod4, mod4, ln_g, ln_b, w_query, sub_keys)


def _extract_top(work, order, tie_rule, out_ref=None, tok=None):
    rank = jnp.full(work.shape, float(PEER_TOPK), F32)
    big = float(PEER_N)
    for r in range(PEER_TOPK):
        m = jnp.max(work, axis=0, keepdims=True)
        sel = work == m
        if tie_rule:
            first = jnp.min(jnp.where(sel, order, big), axis=0, keepdims=True)
            sel = order == first
        rank = jnp.where(sel, float(r), rank)
        work = jnp.where(sel, -jnp.inf, work)
        if out_ref is not None:
            out_ref[r:r + 1, tok] = m
    return rank


def _pair_cells(op, u1, u2):
    g = [op(u1[0:1], u2[0:8]), op(u1[0:1], u2[8:16])]
    for a in range(1, 8):
        g.append(op(u1[a:a + 1], u2[0:8]))
    g.append(op(u1[8:16], u2[0:1]))
    return jnp.concatenate(g, axis=0)


def _topk_body(st_ref, lc_ref, f1_ref, r2_ref, f2_ref, t_ref, rk_ref, l_ref):
    tl = st_ref.shape[-1]
    chunks = [slice(c * LANES, (c + 1) * LANES) for c in range(tl // LANES)]
    key_id = lax.broadcasted_iota(jnp.int32, (PEER_NKEYS, LANES), 0).astype(F32)

    def half_ranks(tie_rule):
        for tok in chunks:
            for p in range(2):
                rk_ref[p, :, tok] = _extract_top(st_ref[p, :, tok], key_id, tie_rule,
                                                 t_ref.at[p], tok)

    half_ranks(False)
    taken = jnp.where(rk_ref[...] < float(PEER_TOPK), 1.0, 0.0)
    most = jnp.max(jnp.sum(taken, axis=1, keepdims=True))

    @pl.when(most > float(PEER_TOPK))
    def _():
        half_ranks(True)

    row = lax.broadcasted_iota(jnp.int32, (80, LANES), 0)
    mid = row - 16
    cell_id = jnp.where(row < 16, row,
                        jnp.where(row < 72,
                                  (jnp.right_shift(mid, 3) + 1) * 16 + jnp.bitwise_and(mid, 7),
                                  (row - 64) * 16)).astype(F32)
    for tok in chunks:
        s1 = st_ref[0, :, tok]
        s2 = st_ref[1, :, tok]
        rank1 = rk_ref[0, :, tok]
        rank2 = rk_ref[1, :, tok]
        t1 = t_ref[0, :, tok]
        t2 = t_ref[1, :, tok]
        crank = _extract_top(_pair_cells(jnp.add, t1, t2), cell_id, True)
        chosen = jnp.where(crank < float(PEER_TOPK), 1.0, 0.0)
        e1 = jnp.exp(t1 - t1[0:1])
        e2 = jnp.exp(t2 - t2[0:1])
        z = jnp.sum(chosen * _pair_cells(jnp.multiply, e1, e2), axis=0, keepdims=True)
        inv_z = 1.0 / z
        l_ref[0:1, tok] = jnp.sum(chosen[0:16], axis=0, keepdims=True)
        for a in range(1, 8):
            l_ref[a:a + 1, tok] = jnp.sum(chosen[8 + 8 * a:16 + 8 * a], axis=0, keepdims=True)
        l_ref[8:16, tok] = chosen[72:80]
        lcnt = l_ref[:, tok]
        lc = jnp.zeros((PEER_NKEYS, LANES), F32)
        for a in range(PEER_TOPK):
            lc = jnp.where(rank1 == float(a), lcnt[a:a + 1], lc)
        lc_ref[:, tok] = lc
        f1_ref[:, tok] = jnp.exp(s1 - t1[0:1]) * inv_z
        r2_ref[:, tok] = rank2
        f2_ref[:, tok] = jnp.where(rank2 < float(PEER_TOPK), jnp.exp(s2 - t2[0:1]), 0.0)


def _topk(st):
    _, nk, t = st.shape
    tl = min(TOPK_TL, t)
    spec = pl.BlockSpec((None, nk, tl), lambda i, h: (h, 0, i))
    shp = jax.ShapeDtypeStruct((PEER_HEADS, nk, t), F32)
    return pl.pallas_call(
        _topk_body,
        grid=(t // tl, PEER_HEADS),
        in_specs=[pl.BlockSpec((2, nk, tl), lambda i, h: (h, 0, i))],
        out_specs=[spec, spec, spec, spec],
        out_shape=[shp, shp, shp, shp],
        scratch_shapes=[pltpu.VMEM((2, PEER_TOPK, tl), F32), pltpu.VMEM((2, nk, tl), F32),
                        pltpu.VMEM((PEER_TOPK, tl), F32)],
        compiler_params=_cparams("arbitrary", "arbitrary"),
        name="topk",
    )(st)


def _peer_body(h2_ref, ed_ref, eut_ref, lc_ref, f1_ref, r2_ref, f2_ref, x1_ref, gf_ref,
               g2_ref, b2_ref, o_ref, acc_ref, at_ref, *, ni):
    et = pl.program_id(1)

    @pl.when(et == 0)
    def _():
        acc_ref[...] = jnp.zeros(acc_ref.shape, F32)

    tm = h2_ref.shape[0]
    sub = BF16_SUBLANES
    kgroup = 2 * sub
    at_ref[...] = lax.dot_general(ed_ref[...], h2_ref[...], _NT, preferred_element_type=F32)
    cols = []
    for c in range(tm // LANES):
        tok = slice(c * LANES, (c + 1) * LANES)
        w_rows = {}
        for k0 in range(0, PEER_NKEYS, kgroup):
            subs = [slice(k0 + j * sub, k0 + (j + 1) * sub) for j in range(kgroup // sub)]
            rank = [[r2_ref[h, rk, tok].astype(BF16) for rk in subs] for h in range(PEER_HEADS)]
            val = [[f2_ref[h, rk, tok].astype(BF16) for rk in subs] for h in range(PEER_HEADS)]
            for ii in range(ni):
                g = [None] * len(subs)
                for h in range(PEER_HEADS):
                    cnt = jnp.broadcast_to(lc_ref[h, ii:ii + 1, tok], (sub, LANES)).astype(BF16)
                    fac = jnp.broadcast_to(f1_ref[h, ii:ii + 1, tok], (sub, LANES)).astype(BF16)
                    for j in range(len(subs)):
                        term = jnp.where(rank[h][j] < cnt, val[h][j], 0) * fac
                        g[j] = term if g[j] is None else g[j] + term
                for j, rk in enumerate(subs):
                    r0 = ii * PEER_NKEYS + rk.start
                    a = at_ref[r0:r0 + sub, tok]
                    act = 0.5 * a * (1.0 + lax.erf(a * (2.0 ** -0.5)))
                    w_rows[r0] = g[j] * act.astype(BF16)
        cols.append(jnp.concatenate([w_rows[r0] for r0 in sorted(w_rows)], axis=0))
    wt = jnp.concatenate(cols, axis=1)
    acc_ref[...] += jnp.dot(eut_ref[...], wt, preferred_element_type=F32)

    @pl.when(et == pl.num_programs(1) - 1)
    def _():
        f = acc_ref[...].T
        y = DN_ALPHA * x1_ref[...] + (1.0 + gf_ref[...]) * f
        o_ref[...] = _ln(y) * g2_ref[...] + b2_ref[...]


def _peer(h2, ed, eut, lc, f1, r2, f2, x1, mod4, ln_g, ln_b, seq):
    t, d = h2.shape
    tm = min(PEER_TM, t)
    te = PEER_TE
    ni = te // PEER_NKEYS
    ne = PEER_N // te
    tpb = seq // tm
    lc4 = lc.reshape(PEER_HEADS, ne, ni, t)
    f14 = f1.reshape(PEER_HEADS, ne, ni, t)
    rowspec = pl.BlockSpec((PEER_HEADS, None, ni, tm), lambda i, e: (0, e, 0, i))
    colspec = pl.BlockSpec((PEER_HEADS, PEER_NKEYS, tm), lambda i, e: (0, 0, i))
    full = lambda *s: pl.BlockSpec(s, lambda i, e: (0,) * len(s))
    return pl.pallas_call(
        functools.partial(_peer_body, ni=ni),
        grid=(t // tm, ne),
        in_specs=[pl.BlockSpec((tm, d), lambda i, e: (i, 0)),
                  pl.BlockSpec((te, d), lambda i, e: (e, 0)),
                  pl.BlockSpec((d, te), lambda i, e: (0, e)),
                  rowspec, rowspec, colspec, colspec,
                  pl.BlockSpec((tm, d), lambda i, e: (i, 0)),
                  pl.BlockSpec((None, None, 1, d), lambda i, e: (i // tpb, 5, 0, 0)),
                  full(1, d), full(1, d)],
        out_specs=pl.BlockSpec((tm, d), lambda i, e: (i, 0)),
        out_shape=jax.ShapeDtypeStruct((t, d), F32),
        scratch_shapes=[pltpu.VMEM((d, tm), F32), pltpu.VMEM((te, tm), F32)],
        compiler_params=_cparams("arbitrary", "arbitrary"),
        name="peer",
    )(h2, ed, eut, lc4, f14, r2, f2, x1, mod4, ln_g, ln_b)


def kernel(x, c, w_ada, b_ada, w_in, w_out, na_rpb, t5_bias, lambda_q1, lambda_k1, lambda_q2,
           lambda_k2, diff_subln, ln1_g, ln1_b, w_query, sub_keys, expert_down, expert_up,
           ln2_g, ln2_b):
    b, seq, d = x.shape
    t = b * seq
    rows = seq // GRID_W
    assert d == D_MODEL and w_ada.shape[0] == DEPTH == 1
    assert rows % NA_QROWS == 0 and rows >= NA_KROWS and seq % DIFF_TQ == 0
    l = 0
    lambda_init = 0.8 - 0.6 * math.exp(-0.3 * l)
    xf = x.reshape(t, d)

    mod4 = _ada(c, w_ada[l], b_ada[l][None, :]).reshape(b, 6, 1, d)
    proj3 = _proj(xf, mod4, w_in[l].astype(BF16), seq).reshape(b, seq, IN_WIDTH)

    nb = _na_bias(na_rpb[l].reshape(-1), rows)
    o_na = _na(proj3, nb, rows).reshape(t, NA_WIDTH)

    dt = _t5_bias(t5_bias.reshape(-1), seq)
    o_df = _diff(proj3, dt, lambda_q1[l][None, :], lambda_k1[l][None, :], lambda_q2[l][None, :],
                 lambda_k2[l][None, :], diff_subln[l][None, :], lambda_init).reshape(t, DIFF_WIDTH)

    x1, h2, st = _mid(o_na, o_df, w_out[l].astype(BF16), xf, mod4, ln1_g[l][None, :],
                      ln1_b[l][None, :], w_query[l].astype(BF16), sub_keys[l].astype(BF16), seq)
    lc, f1, r2, f2 = _topk(st)
    out = _peer(h2, expert_down[l].astype(BF16), expert_up[l].T.astype(BF16), lc, f1, r2, f2,
                x1, mod4, ln2_g[l][None, :], ln2_b[l][None, :], seq)
    return out.reshape(b, seq, d)
```

```python
import functools
import math

import jax
import jax.numpy as jnp
from jax import lax
from jax.experimental import pallas as pl
from jax.experimental.pallas import tpu as pltpu

F32 = jnp.float32
BF16 = jnp.bfloat16

D_MODEL = 1024
DEPTH = 1
GRID_W = 64
NA_HEADS = 8
NA_HEAD_DIM = 64
NA_WIDTH = NA_HEADS * NA_HEAD_DIM
NA_WIN_ROWS = 8
NA_WIN_COLS = 16
DIFF_HEADS = 4
DIFF_HEAD_DIM = 64
DIFF_V_DIM = 2 * DIFF_HEAD_DIM
DIFF_WIDTH = DIFF_HEADS * DIFF_V_DIM
IN_WIDTH = 3 * NA_WIDTH + 3 * DIFF_WIDTH
T5_BUCKETS = 32
T5_MAX_DIST = 128
PEER_HEADS = 8
PEER_NKEYS = 128
PEER_N = PEER_NKEYS * PEER_NKEYS
PEER_TOPK = 16
LN_EPS = 1e-5
DN_ALPHA = (2.0 * DEPTH) ** 0.25
NEG_BIG = -1e30
LANES = 128
BF16_SUBLANES = 16

PROJ_TM = 256
NA_QROWS = 4
NA_KROWS = 12
DIFF_TQ = 256
DIFF_TILE = 256
MID_TM = 256
TOPK_TL = 1024
PEER_TM = 512
PEER_TE = 2048
VMEM_LIMIT = 56 * 1024 * 1024

_NT = (((1,), (1,)), ((), ()))


def _cparams(*sem):
    return pltpu.CompilerParams(dimension_semantics=sem, vmem_limit_bytes=VMEM_LIMIT)


def _ln(x):
    mu = jnp.mean(x, axis=-1, keepdims=True)
    xc = x - mu
    var = jnp.mean(xc * xc, axis=-1, keepdims=True)
    return xc * lax.rsqrt(var + LN_EPS)


def _ada_body(c_ref, w_ref, b_ref, o_ref):
    c = c_ref[...]
    s = c * jax.nn.sigmoid(c)
    o_ref[...] = jnp.dot(s, w_ref[...], preferred_element_type=F32) + b_ref[...]


def _ada(c, w, bias):
    b, d = c.shape
    n = w.shape[1]
    return pl.pallas_call(
        _ada_body,
        grid=(n // d,),
        in_specs=[pl.BlockSpec((b, d), lambda j: (0, 0)),
                  pl.BlockSpec((d, d), lambda j: (0, j)),
                  pl.BlockSpec((1, d), lambda j: (0, j))],
        out_specs=pl.BlockSpec((b, d), lambda j: (0, j)),
        out_shape=jax.ShapeDtypeStruct((b, n), F32),
        compiler_params=_cparams("arbitrary"),
        name="ada",
    )(c, w, bias)


def _proj_body(x_ref, sc_ref, sh_ref, w_ref, o_ref):
    h = _ln(x_ref[...]) * (1.0 + sc_ref[...]) + sh_ref[...]
    o_ref[...] = jnp.dot(h.astype(BF16), w_ref[...], preferred_element_type=F32).astype(o_ref.dtype)


def _proj(xf, mod4, w, seq):
    t, d = xf.shape
    n = w.shape[1]
    tpb = seq // PROJ_TM
    return pl.pallas_call(
        _proj_body,
        grid=(t // PROJ_TM,),
        in_specs=[pl.BlockSpec((PROJ_TM, d), lambda i: (i, 0)),
                  pl.BlockSpec((None, None, 1, d), lambda i: (i // tpb, 1, 0, 0)),
                  pl.BlockSpec((None, None, 1, d), lambda i: (i // tpb, 0, 0, 0)),
                  pl.BlockSpec((d, n), lambda i: (0, 0))],
        out_specs=pl.BlockSpec((PROJ_TM, n), lambda i: (i, 0)),
        out_shape=jax.ShapeDtypeStruct((t, n), BF16),
        compiler_params=_cparams("arbitrary"),
        name="proj",
    )(xf, mod4, mod4, w)


def _na_row_offsets(rows):
    kr = min(NA_WIN_ROWS, rows)
    nrb = rows // NA_QROWS
    pats = []
    for rb in range(nrb):
        r0 = rb * NA_QROWS
        s = min(max(r0 - NA_WIN_ROWS // 2, 0), rows - NA_KROWS)
        pat = []
        for qr in range(NA_QROWS):
            r = r0 + qr
            rs = min(max(r - kr // 2, 0), rows - kr)
            pat.append(tuple((s + i) - r + (NA_WIN_ROWS - 1) if rs <= s + i < rs + kr else 15
                             for i in range(NA_KROWS)))
        pats.append(tuple(pat))
    variants = [pats[0], pats[1], pats[-1]]
    assert all(p == variants[1] for p in pats[1:-1])
    return variants


def _na_bias_body(rpb_ref, o_ref, tile_ref, *, variants):
    h = pl.program_id(0)
    nro = 2 * NA_WIN_ROWS - 1
    nco = 2 * NA_WIN_COLS - 1
    q = lax.broadcasted_iota(jnp.int32, (GRID_W, LANES), 0)
    kk = lax.broadcasted_iota(jnp.int32, (GRID_W, LANES), 1)
    k = jnp.bitwise_and(kk, GRID_W - 1)
    col_start = jnp.clip(q - NA_WIN_COLS // 2, 0, GRID_W - NA_WIN_COLS)
    valid = jnp.logical_and(k >= col_start, k < col_start + NA_WIN_COLS)
    co = jnp.clip(k - q, -(NA_WIN_COLS - 1), NA_WIN_COLS - 1) + (NA_WIN_COLS - 1)
    neg = jnp.full((GRID_W, LANES), NEG_BIG, F32)
    for ro in range(nro):
        acc = neg
        for cc in range(nco):
            acc = jnp.where(co == cc, rpb_ref[(h * nro + ro) * nco + cc], acc)
        tile_ref[ro] = jnp.where(valid, acc, NEG_BIG)
    tile_ref[nro] = neg
    left = kk < GRID_W
    for v, pat in enumerate(variants):
        for qr in range(NA_QROWS):
            for kp in range(NA_KROWS // 2):
                a, b = pat[qr][2 * kp], pat[qr][2 * kp + 1]
                o_ref[v, qr * GRID_W:(qr + 1) * GRID_W, kp * LANES:(kp + 1) * LANES] = (
                    jnp.where(left, tile_ref[a], tile_ref[b]))


def _na_bias(rpb_flat, rows):
    variants = _na_row_offsets(rows)
    nq, nk = NA_QROWS * GRID_W, NA_KROWS * GRID_W
    return pl.pallas_call(
        functools.partial(_na_bias_body, variants=variants),
        grid=(NA_HEADS,),
        in_specs=[pl.BlockSpec(memory_space=pltpu.SMEM)],
        out_specs=pl.BlockSpec((3, None, nq, nk), lambda h: (0, h, 0, 0)),
        out_shape=jax.ShapeDtypeStruct((3, NA_HEADS, nq, nk), F32),
        scratch_shapes=[pltpu.VMEM((2 * NA_WIN_ROWS, GRID_W, LANES), F32)],
        compiler_params=_cparams("arbitrary"),
        name="na_bias",
    )(rpb_flat)


def _half_masks(value=1.0):
    lane = lax.broadcasted_iota(jnp.int32, (1, LANES), 1)
    lo = jnp.where(lane < LANES // 2, value, 0.0).astype(BF16)
    hi = jnp.where(lane < LANES // 2, 0.0, value).astype(BF16)
    return lo, hi


def _na_body(q_ref, k_ref, v_ref, b_ref, o_ref, *, rows):
    rb = pl.program_id(1)
    s_row = jnp.clip(rb * NA_QROWS - NA_WIN_ROWS // 2, 0, rows - NA_KROWS)
    start = pl.multiple_of(s_row * GRID_W, GRID_W)
    nk = NA_KROWS * GRID_W
    masks = list(zip(_half_masks(NA_HEAD_DIM ** -0.5), _half_masks()))
    for hp in range(NA_HEADS // 2):
        cols = slice(hp * LANES, (hp + 1) * LANES)
        q = q_ref[:, cols]
        kw = k_ref[pl.ds(start, nk), cols]
        vw = v_ref[pl.ds(start, nk), cols]
        acc = jnp.zeros(q.shape, F32)
        for h, (mq, mv) in enumerate(masks):
            s = lax.dot_general(q * mq, kw, _NT, preferred_element_type=F32) + b_ref[2 * hp + h]
            p = jnp.exp(s - jnp.max(s, axis=-1, keepdims=True))
            l = jnp.sum(p, axis=-1, keepdims=True)
            o = jnp.dot(p.astype(BF16), vw * mv, preferred_element_type=F32)
            acc = acc + o * (1.0 / l)
        o_ref[:, cols] = acc.astype(o_ref.dtype)


def _na(proj3, nb, rows):
    b, seq, _ = proj3.shape
    nrb = rows // NA_QROWS
    nq, nk = NA_QROWS * GRID_W, NA_KROWS * GRID_W

    def variant(rb):
        return jnp.where(rb == 0, 0, jnp.where(rb == nrb - 1, 2, 1))

    return pl.pallas_call(
        functools.partial(_na_body, rows=rows),
        grid=(b, nrb),
        in_specs=[pl.BlockSpec((None, nq, NA_WIDTH), lambda bi, rb: (bi, rb, 0)),
                  pl.BlockSpec((None, seq, NA_WIDTH), lambda bi, rb: (bi, 0, 1)),
                  pl.BlockSpec((None, seq, NA_WIDTH), lambda bi, rb: (bi, 0, 2)),
                  pl.BlockSpec((None, NA_HEADS, nq, nk), lambda bi, rb: (variant(rb), 0, 0, 0))],
        out_specs=pl.BlockSpec((None, nq, NA_WIDTH), lambda bi, rb: (bi, rb, 0)),
        out_shape=jax.ShapeDtypeStruct((b, seq, NA_WIDTH), BF16),
        compiler_params=_cparams("arbitrary", "arbitrary"),
        name="na",
    )(proj3, proj3, proj3, nb)


def _t5_thresholds():
    nb = T5_BUCKETS // 2
    me = nb // 2
    span = nb - me
    ratio = T5_MAX_DIST // me
    out = []
    for m in range(1, span):
        n = me
        while n ** span < me ** span * ratio ** m:
            n += 1
        out.append(n)
    return out


def _t5_bias_body(t5_ref, o_ref, *, nq):
    h = pl.program_id(0)
    nb = T5_BUCKETS // 2
    me = nb // 2
    q = lax.broadcasted_iota(jnp.int32, (DIFF_TILE, DIFF_TILE), 0)
    k = lax.broadcasted_iota(jnp.int32, (DIFF_TILE, DIFF_TILE), 1)
    rel = (pl.program_id(1) - (nq - 1)) * DIFF_TILE + k - q
    n = jnp.abs(rel)
    large = jnp.full(n.shape, me, jnp.int32)
    for thr in _t5_thresholds():
        large = large + jnp.where(n >= thr, 1, 0)
    bkt = jnp.where(rel > 0, nb, 0) + jnp.where(n < me, n, large)
    acc = jnp.zeros(bkt.shape, F32)
    for bb in range(T5_BUCKETS):
        acc = jnp.where(bkt == bb, t5_ref[bb * DIFF_HEADS + h], acc)
    o_ref[...] = acc


def _t5_bias(t5_flat, seq):
    nq = seq // DIFF_TILE
    nt = 2 * nq - 1
    return pl.pallas_call(
        functools.partial(_t5_bias_body, nq=nq),
        grid=(DIFF_HEADS, nt),
        in_specs=[pl.BlockSpec(memory_space=pltpu.SMEM)],
        out_specs=pl.BlockSpec((None, None, DIFF_TILE, DIFF_TILE), lambda h, t: (h, t, 0, 0)),
        out_shape=jax.ShapeDtypeStruct((DIFF_HEADS, nt, DIFF_TILE, DIFF_TILE), F32),
        compiler_params=_cparams("arbitrary", "arbitrary"),
        name="t5_bias",
    )(t5_flat)


def _diff_body(q_ref, k_ref, v_ref, dt_ref, lq1_ref, lk1_ref, lq2_ref, lk2_ref, g_ref, o_ref,
               *, nkb, lambda_init):
    qi = pl.program_id(2)
    q = q_ref[...]
    nsub = DIFF_TQ // DIFF_TILE
    lam = (jnp.exp(jnp.sum(lq1_ref[...] * lk1_ref[...], axis=-1, keepdims=True))
           - jnp.exp(jnp.sum(lq2_ref[...] * lk2_ref[...], axis=-1, keepdims=True)) + lambda_init)
    maps = []
    for m in _half_masks(DIFF_HEAD_DIM ** -0.5):
        qm = q * m
        pieces = []
        for kb in range(nkb):
            s = lax.dot_general(qm, k_ref[kb * DIFF_TILE:(kb + 1) * DIFF_TILE, :], _NT,
                                preferred_element_type=F32)
            bias = [dt_ref[nkb - 1 - (nsub * qi + hq) + kb] for hq in range(nsub)]
            pieces.append(s + jnp.concatenate(bias, axis=0))
        s = jnp.concatenate(pieces, axis=1)
        p = jnp.exp(s - jnp.max(s, axis=-1, keepdims=True))
        maps.append((p, jnp.sum(p, axis=-1, keepdims=True)))
    (p1, l1), (p2, l2) = maps
    w = p1 * (1.0 / l1) - p2 * (lam / l2)
    o = jnp.dot(w.astype(BF16), v_ref[...], preferred_element_type=F32)
    o = o * lax.rsqrt(jnp.mean(o * o, axis=-1, keepdims=True) + LN_EPS) * g_ref[...]
    o_ref[...] = (o * (1.0 - lambda_init)).astype(o_ref.dtype)


def _diff(proj3, dt, lq1, lk1, lq2, lk2, subln, lambda_init):
    b, seq, _ = proj3.shape
    nkb = seq // DIFF_TILE
    nt = 2 * nkb - 1
    c0 = 3 * NA_WIDTH // LANES
    vec = lambda n: pl.BlockSpec((1, n), lambda h, bi, qi: (0, 0))
    return pl.pallas_call(
        functools.partial(_diff_body, nkb=nkb, lambda_init=lambda_init),
        grid=(DIFF_HEADS, b, seq // DIFF_TQ),
        in_specs=[pl.BlockSpec((None, DIFF_TQ, LANES), lambda h, bi, qi: (bi, qi, c0 + h)),
                  pl.BlockSpec((None, seq, LANES), lambda h, bi, qi: (bi, 0, c0 + DIFF_HEADS + h)),
                  pl.BlockSpec((None, seq, LANES), lambda h, bi, qi: (bi, 0, c0 + 2 * DIFF_HEADS + h)),
                  pl.BlockSpec((None, nt, DIFF_TILE, DIFF_TILE), lambda h, bi, qi: (h, 0, 0, 0)),
                  vec(DIFF_HEAD_DIM), vec(DIFF_HEAD_DIM), vec(DIFF_HEAD_DIM), vec(DIFF_HEAD_DIM),
                  vec(DIFF_V_DIM)],
        out_specs=pl.BlockSpec((None, DIFF_TQ, LANES), lambda h, bi, qi: (bi, qi, h)),
        out_shape=jax.ShapeDtypeStruct((b, seq, DIFF_WIDTH), BF16),
        compiler_params=_cparams("arbitrary", "arbitrary", "arbitrary"),
        name="diff",
    )(proj3, proj3, proj3, dt, lq1, lk1, lq2, lk2, subln)


def _mid_body(ona_ref, odf_ref, wout_ref, x_ref, ga_ref, scf_ref, shf_ref, g1_ref, b1_ref,
              wq_ref, sk_ref, x1_ref, h2_ref, st_ref):
    mix = (jnp.dot(ona_ref[...], wout_ref[0:NA_WIDTH, :], preferred_element_type=F32)
           + jnp.dot(odf_ref[...], wout_ref[NA_WIDTH:, :], preferred_element_type=F32))
    y = DN_ALPHA * x_ref[...] + (1.0 + ga_ref[...]) * mix
    x1 = _ln(y) * g1_ref[...] + b1_ref[...]
    x1_ref[...] = x1
    h2 = (_ln(x1) * (1.0 + scf_ref[...]) + shf_ref[...]).astype(BF16)
    h2_ref[...] = h2
    qp = jnp.dot(h2, wq_ref[...], preferred_element_type=F32).astype(BF16)
    for hp in range(2 * PEER_HEADS):
        st_ref[hp] = lax.dot_general(sk_ref[hp % 2], qp[:, hp * PEER_NKEYS:(hp + 1) * PEER_NKEYS],
                                     _NT, preferred_element_type=F32)


def _mid(o_na, o_df, w_out, xf, mod4, ln_g, ln_b, w_query, sub_keys, seq):
    t, d = xf.shape
    tpb = seq // MID_TM
    nqk = w_query.shape[1]
    modspec = lambda j: pl.BlockSpec((None, None, 1, d), lambda i: (i // tpb, j, 0, 0))
    full = lambda *s: pl.BlockSpec(s, lambda i: (0,) * len(s))
    return pl.pallas_call(
        _mid_body,
        grid=(t // MID_TM,),
        in_specs=[pl.BlockSpec((MID_TM, NA_WIDTH), lambda i: (i, 0)),
                  pl.BlockSpec((MID_TM, DIFF_WIDTH), lambda i: (i, 0)),
                  full(d, d),
                  pl.BlockSpec((MID_TM, d), lambda i: (i, 0)),
                  modspec(2), modspec(4), modspec(3),
                  full(1, d), full(1, d),
                  full(d, nqk),
                  full(2, PEER_NKEYS, PEER_NKEYS)],
        out_specs=[pl.BlockSpec((MID_TM, d), lambda i: (i, 0)),
                   pl.BlockSpec((MID_TM, d), lambda i: (i, 0)),
                   pl.BlockSpec((2 * PEER_HEADS, PEER_NKEYS, MID_TM), lambda i: (0, 0, i))],
        out_shape=[jax.ShapeDtypeStruct((t, d), F32),
                   jax.ShapeDtypeStruct((t, d), BF16),
                   jax.ShapeDtypeStruct((2 * PEER_HEADS, PEER_NKEYS, t), F32)],
        compiler_params=_cparams("arbitrary"),
        name="mid",
    )(o_na, o_df, w_out, xf, mod4, mod4, mod4, ln_g, ln_b, w_query, sub_keys)


def _extract_top(work, order, tie_rule, out_ref=None, tok=None):
    rank = jnp.full(work.shape, float(PEER_TOPK), F32)
    big = float(PEER_N)
    for r in range(PEER_TOPK):
        m = jnp.max(work, axis=0, keepdims=True)
        sel = work == m
        if tie_rule:
            first = jnp.min(jnp.where(sel, order, big), axis=0, keepdims=True)
            sel = order == first
        rank = jnp.where(sel, float(r), rank)
        work = jnp.where(sel, -jnp.inf, work)
        if out_ref is not None:
            out_ref[r:r + 1, tok] = m
    return rank


def _pair_cells(op, u1, u2):
    g = [op(u1[0:1], u2[0:8]), op(u1[0:1], u2[8:16])]
    for a in range(1, 8):
        g.append(op(u1[a:a + 1], u2[0:8]))
    g.append(op(u1[8:16], u2[0:1]))
    return jnp.concatenate(g, axis=0)


def _topk_body(st_ref, lc_ref, f1_ref, r2_ref, f2_ref, t_ref, rk_ref, l_ref, z_ref):
    tl = st_ref.shape[-1]
    chunks = [slice(c * LANES, (c + 1) * LANES) for c in range(tl // LANES)]
    key_id = lax.broadcasted_iota(jnp.int32, (PEER_NKEYS, LANES), 0).astype(F32)

    def half_ranks(tie_rule):
        for tok in chunks:
            for p in range(2):
                rk_ref[p, :, tok] = _extract_top(st_ref[p, :, tok], key_id, tie_rule,
                                                 t_ref.at[p], tok)

    half_ranks(False)
    taken = jnp.where(rk_ref[...] < float(PEER_TOPK), 1.0, 0.0)
    most = jnp.max(jnp.sum(taken, axis=1, keepdims=True))

    @pl.when(most > float(PEER_TOPK))
    def _():
        half_ranks(True)

    wide = 2 * LANES
    row = lax.broadcasted_iota(jnp.int32, (80, wide), 0)
    mid = row - 16
    cell_id = jnp.where(row < 16, row,
                        jnp.where(row < 72,
                                  (jnp.right_shift(mid, 3) + 1) * 16 + jnp.bitwise_and(mid, 7),
                                  (row - 64) * 16)).astype(F32)
    for c0 in range(0, tl, wide):
        tok2 = slice(c0, c0 + wide)
        t1 = t_ref[0, :, tok2]
        t2 = t_ref[1, :, tok2]
        work = _pair_cells(jnp.add, t1, t2)
        for _ in range(PEER_TOPK):
            m = jnp.max(work, axis=0, keepdims=True)
            first = jnp.min(jnp.where(work == m, cell_id, float(PEER_N)), axis=0, keepdims=True)
            work = jnp.where(cell_id == first, -jnp.inf, work)
        chosen = jnp.where(work == -jnp.inf, 1.0, 0.0)
        e1 = jnp.exp(t1 - t1[0:1])
        e2 = jnp.exp(t2 - t2[0:1])
        z = jnp.sum(chosen * _pair_cells(jnp.multiply, e1, e2), axis=0, keepdims=True)
        z_ref[0:1, tok2] = 1.0 / z
        l_ref[0:1, tok2] = jnp.sum(chosen[0:16], axis=0, keepdims=True)
        for a in range(1, 8):
            l_ref[a:a + 1, tok2] = jnp.sum(chosen[8 + 8 * a:16 + 8 * a], axis=0, keepdims=True)
        l_ref[8:16, tok2] = chosen[72:80]
    for tok in chunks:
        rank1 = rk_ref[0, :, tok]
        rank2 = rk_ref[1, :, tok]
        lcnt = l_ref[:, tok]
        lc = jnp.zeros((PEER_NKEYS, LANES), F32)
        for a in range(PEER_TOPK):
            lc = jnp.where(rank1 == float(a), lcnt[a:a + 1], lc)
        lc_ref[:, tok] = lc
        f1_ref[:, tok] = jnp.exp(st_ref[0, :, tok] - t_ref[0, 0:1, tok]) * z_ref[0:1, tok]
        r2_ref[:, tok] = rank2
        f2_ref[:, tok] = jnp.where(rank2 < float(PEER_TOPK),
                                   jnp.exp(st_ref[1, :, tok] - t_ref[1, 0:1, tok]), 0.0)


def _topk(st):
    _, nk, t = st.shape
    tl = min(TOPK_TL, t)
    spec = pl.BlockSpec((None, nk, tl), lambda i, h: (h, 0, i))
    shp = jax.ShapeDtypeStruct((PEER_HEADS, nk, t), F32)
    return pl.pallas_call(
        _topk_body,
        grid=(t // tl, PEER_HEADS),
        in_specs=[pl.BlockSpec((2, nk, tl), lambda i, h: (h, 0, i))],
        out_specs=[spec, spec, spec, spec],
        out_shape=[shp, shp, shp, shp],
        scratch_shapes=[pltpu.VMEM((2, PEER_TOPK, tl), F32), pltpu.VMEM((2, nk, tl), F32),
                        pltpu.VMEM((PEER_TOPK, tl), F32), pltpu.VMEM((8, tl), F32)],
        compiler_params=_cparams("arbitrary", "arbitrary"),
        name="topk",
    )(st)


def _peer_body(h2_ref, ed_ref, eut_ref, lc_ref, f1_ref, r2_ref, f2_ref, x1_ref, gf_ref,
               g2_ref, b2_ref, o_ref, acc_ref, at_ref, *, ni):
    et = pl.program_id(1)

    @pl.when(et == 0)
    def _():
        acc_ref[...] = jnp.zeros(acc_ref.shape, F32)

    tm = h2_ref.shape[0]
    sub = BF16_SUBLANES
    kgroup = 2 * sub
    at_ref[...] = lax.dot_general(ed_ref[...], h2_ref[...], _NT, preferred_element_type=F32)
    cols = []
    for c in range(tm // LANES):
        tok = slice(c * LANES, (c + 1) * LANES)
        w_rows = {}
        for k0 in range(0, PEER_NKEYS, kgroup):
            subs = [slice(k0 + j * sub, k0 + (j + 1) * sub) for j in range(kgroup // sub)]
            rank = [[r2_ref[h, rk, tok].astype(BF16) for rk in subs] for h in range(PEER_HEADS)]
            val = [[f2_ref[h, rk, tok].astype(BF16) for rk in subs] for h in range(PEER_HEADS)]
            for ii in range(ni):
                g = [None] * len(subs)
                for h in range(PEER_HEADS):
                    cnt = jnp.broadcast_to(lc_ref[h, ii:ii + 1, tok], (sub, LANES)).astype(BF16)
                    fac = jnp.broadcast_to(f1_ref[h, ii:ii + 1, tok], (sub, LANES)).astype(BF16)
                    for j in range(len(subs)):
                        term = jnp.where(rank[h][j] < cnt, val[h][j], 0) * fac
                        g[j] = term if g[j] is None else g[j] + term
                for j, rk in enumerate(subs):
                    r0 = ii * PEER_NKEYS + rk.start
                    a = at_ref[r0:r0 + sub, tok]
                    act = 0.5 * a * (1.0 + lax.erf(a * (2.0 ** -0.5)))
                    w_rows[r0] = g[j] * act.astype(BF16)
        cols.append(jnp.concatenate([w_rows[r0] for r0 in sorted(w_rows)], axis=0))
    wt = jnp.concatenate(cols, axis=1)
    acc_ref[...] += jnp.dot(eut_ref[...], wt, preferred_element_type=F32)

    @pl.when(et == pl.num_programs(1) - 1)
    def _():
        f = acc_ref[...].T
        y = DN_ALPHA * x1_ref[...] + (1.0 + gf_ref[...]) * f
        o_ref[...] = _ln(y) * g2_ref[...] + b2_ref[...]


def _peer(h2, ed, eut, lc, f1, r2, f2, x1, mod4, ln_g, ln_b, seq):
    t, d = h2.shape
    tm = min(PEER_TM, t)
    te = PEER_TE
    ni = te // PEER_NKEYS
    ne = PEER_N // te
    tpb = seq // tm
    lc4 = lc.reshape(PEER_HEADS, ne, ni, t)
    f14 = f1.reshape(PEER_HEADS, ne, ni, t)
    rowspec = pl.BlockSpec((PEER_HEADS, None, ni, tm), lambda i, e: (0, e, 0, i))
    colspec = pl.BlockSpec((PEER_HEADS, PEER_NKEYS, tm), lambda i, e: (0, 0, i))
    full = lambda *s: pl.BlockSpec(s, lambda i, e: (0,) * len(s))
    return pl.pallas_call(
        functools.partial(_peer_body, ni=ni),
        grid=(t // tm, ne),
        in_specs=[pl.BlockSpec((tm, d), lambda i, e: (i, 0)),
                  pl.BlockSpec((te, d), lambda i, e: (e, 0)),
                  pl.BlockSpec((d, te), lambda i, e: (0, e)),
                  rowspec, rowspec, colspec, colspec,
                  pl.BlockSpec((tm, d), lambda i, e: (i, 0)),
                  pl.BlockSpec((None, None, 1, d), lambda i, e: (i // tpb, 5, 0, 0)),
                  full(1, d), full(1, d)],
        out_specs=pl.BlockSpec((tm, d), lambda i, e: (i, 0)),
        out_shape=jax.ShapeDtypeStruct((t, d), F32),
        scratch_shapes=[pltpu.VMEM((d, tm), F32), pltpu.VMEM((te, tm), F32)],
        compiler_params=_cparams("arbitrary", "arbitrary"),
        name="peer",
    )(h2, ed, eut, lc4, f14, r2, f2, x1, mod4, ln_g, ln_b)


def kernel(x, c, w_ada, b_ada, w_in, w_out, na_rpb, t5_bias, lambda_q1, lambda_k1, lambda_q2,
           lambda_k2, diff_subln, ln1_g, ln1_b, w_query, sub_keys, expert_down, expert_up,
           ln2_g, ln2_b):
    b, seq, d = x.shape
    t = b * seq
    rows = seq // GRID_W
    assert d == D_MODEL and w_ada.shape[0] == DEPTH == 1
    assert rows % NA_QROWS == 0 and rows >= NA_KROWS and seq % DIFF_TQ == 0
    l = 0
    lambda_init = 0.8 - 0.6 * math.exp(-0.3 * l)
    xf = x.reshape(t, d)

    mod4 = _ada(c, w_ada[l], b_ada[l][None, :]).reshape(b, 6, 1, d)
    proj3 = _proj(xf, mod4, w_in[l].astype(BF16), seq).reshape(b, seq, IN_WIDTH)

    nb = _na_bias(na_rpb[l].reshape(-1), rows)
    o_na = _na(proj3, nb, rows).reshape(t, NA_WIDTH)

    dt = _t5_bias(t5_bias.reshape(-1), seq)
    o_df = _diff(proj3, dt, lambda_q1[l][None, :], lambda_k1[l][None, :], lambda_q2[l][None, :],
                 lambda_k2[l][None, :], diff_subln[l][None, :], lambda_init).reshape(t, DIFF_WIDTH)

    x1, h2, st = _mid(o_na, o_df, w_out[l].astype(BF16), xf, mod4, ln1_g[l][None, :],
                      ln1_b[l][None, :], w_query[l].astype(BF16), sub_keys[l].astype(BF16), seq)
    lc, f1, r2, f2 = _topk(st)
    out = _peer(h2, expert_down[l].astype(BF16), expert_up[l].astype(BF16).T, lc, f1, r2, f2,
                x1, mod4, ln2_g[l][None, :], ln2_b[l][None, :], seq)
    return out.reshape(b, seq, d)
```

```python
import functools
import math

import jax
import jax.numpy as jnp
from jax import lax
from jax.experimental import pallas as pl
from jax.experimental.pallas import tpu as pltpu

F32 = jnp.float32
BF16 = jnp.bfloat16

D_MODEL = 1024
DEPTH = 1
GRID_W = 64
NA_HEADS = 8
NA_HEAD_DIM = 64
NA_WIDTH = NA_HEADS * NA_HEAD_DIM
NA_WIN_ROWS = 8
NA_WIN_COLS = 16
DIFF_HEADS = 4
DIFF_HEAD_DIM = 64
DIFF_V_DIM = 2 * DIFF_HEAD_DIM
DIFF_WIDTH = DIFF_HEADS * DIFF_V_DIM
IN_WIDTH = 3 * NA_WIDTH + 3 * DIFF_WIDTH
T5_BUCKETS = 32
T5_MAX_DIST = 128
PEER_HEADS = 8
PEER_NKEYS = 128
PEER_N = PEER_NKEYS * PEER_NKEYS
PEER_TOPK = 16
LN_EPS = 1e-5
DN_ALPHA = (2.0 * DEPTH) ** 0.25
NEG_BIG = -1e30
LANES = 128
BF16_SUBLANES = 16

PROJ_TM = 512
NA_QROWS = 4
NA_KROWS = 12
DIFF_TQ = 256
DIFF_TILE = 256
MID_TM = 512
TOPK_TL = 1024
PEER_TM = 512
PEER_TE = 2048
VMEM_LIMIT = 56 * 1024 * 1024

_NT = (((1,), (1,)), ((), ()))


def _cparams(*sem):
    return pltpu.CompilerParams(dimension_semantics=sem, vmem_limit_bytes=VMEM_LIMIT)


def _ln(x):
    mu = jnp.mean(x, axis=-1, keepdims=True)
    xc = x - mu
    var = jnp.mean(xc * xc, axis=-1, keepdims=True)
    return xc * lax.rsqrt(var + LN_EPS)


def _ada_body(c_ref, w_ref, b_ref, o_ref):
    c = c_ref[...]
    s = c * jax.nn.sigmoid(c)
    o_ref[...] = jnp.dot(s, w_ref[...], preferred_element_type=F32) + b_ref[...]


def _ada(c, w, bias):
    b, d = c.shape
    n = w.shape[1]
    return pl.pallas_call(
        _ada_body,
        grid=(n // d,),
        in_specs=[pl.BlockSpec((b, d), lambda j: (0, 0)),
                  pl.BlockSpec((d, d), lambda j: (0, j)),
                  pl.BlockSpec((1, d), lambda j: (0, j))],
        out_specs=pl.BlockSpec((b, d), lambda j: (0, j)),
        out_shape=jax.ShapeDtypeStruct((b, n), F32),
        compiler_params=_cparams("arbitrary"),
        name="ada",
    )(c, w, bias)


def _proj_body(x_ref, sc_ref, sh_ref, w_ref, o_ref):
    h = _ln(x_ref[...]) * (1.0 + sc_ref[...]) + sh_ref[...]
    o_ref[...] = jnp.dot(h.astype(BF16), w_ref[...], preferred_element_type=F32).astype(o_ref.dtype)


def _proj(xf, mod4, w, seq):
    t, d = xf.shape
    n = w.shape[1]
    tpb = seq // PROJ_TM
    return pl.pallas_call(
        _proj_body,
        grid=(t // PROJ_TM,),
        in_specs=[pl.BlockSpec((PROJ_TM, d), lambda i: (i, 0)),
                  pl.BlockSpec((None, None, 1, d), lambda i: (i // tpb, 1, 0, 0)),
                  pl.BlockSpec((None, None, 1, d), lambda i: (i // tpb, 0, 0, 0)),
                  pl.BlockSpec((d, n), lambda i: (0, 0))],
        out_specs=pl.BlockSpec((PROJ_TM, n), lambda i: (i, 0)),
        out_shape=jax.ShapeDtypeStruct((t, n), BF16),
        compiler_params=_cparams("arbitrary"),
        name="proj",
    )(xf, mod4, mod4, w)


def _na_row_offsets(rows):
    kr = min(NA_WIN_ROWS, rows)
    nrb = rows // NA_QROWS
    pats = []
    for rb in range(nrb):
        r0 = rb * NA_QROWS
        s = min(max(r0 - NA_WIN_ROWS // 2, 0), rows - NA_KROWS)
        pat = []
        for qr in range(NA_QROWS):
            r = r0 + qr
            rs = min(max(r - kr // 2, 0), rows - kr)
            pat.append(tuple((s + i) - r + (NA_WIN_ROWS - 1) if rs <= s + i < rs + kr else 15
                             for i in range(NA_KROWS)))
        pats.append(tuple(pat))
    variants = [pats[0], pats[1], pats[-1]]
    assert all(p == variants[1] for p in pats[1:-1])
    return variants


def _na_bias_body(rpb_ref, o_ref, tile_ref, *, variants):
    h = pl.program_id(0)
    nro = 2 * NA_WIN_ROWS - 1
    nco = 2 * NA_WIN_COLS - 1
    q = lax.broadcasted_iota(jnp.int32, (GRID_W, LANES), 0)
    kk = lax.broadcasted_iota(jnp.int32, (GRID_W, LANES), 1)
    k = jnp.bitwise_and(kk, GRID_W - 1)
    col_start = jnp.clip(q - NA_WIN_COLS // 2, 0, GRID_W - NA_WIN_COLS)
    valid = jnp.logical_and(k >= col_start, k < col_start + NA_WIN_COLS)
    co = jnp.clip(k - q, -(NA_WIN_COLS - 1), NA_WIN_COLS - 1) + (NA_WIN_COLS - 1)
    neg = jnp.full((GRID_W, LANES), NEG_BIG, F32)
    for ro in range(nro):
        acc = neg
        for cc in range(nco):
            acc = jnp.where(co == cc, rpb_ref[(h * nro + ro) * nco + cc], acc)
        tile_ref[ro] = jnp.where(valid, acc, NEG_BIG)
    tile_ref[nro] = neg
    left = kk < GRID_W
    for v, pat in enumerate(variants):
        for qr in range(NA_QROWS):
            for kp in range(NA_KROWS // 2):
                a, b = pat[qr][2 * kp], pat[qr][2 * kp + 1]
                o_ref[v, qr * GRID_W:(qr + 1) * GRID_W, kp * LANES:(kp + 1) * LANES] = (
                    jnp.where(left, tile_ref[a], tile_ref[b]))


def _na_bias(rpb_flat, rows):
    variants = _na_row_offsets(rows)
    nq, nk = NA_QROWS * GRID_W, NA_KROWS * GRID_W
    return pl.pallas_call(
        functools.partial(_na_bias_body, variants=variants),
        grid=(NA_HEADS,),
        in_specs=[pl.BlockSpec(memory_space=pltpu.SMEM)],
        out_specs=pl.BlockSpec((3, None, nq, nk), lambda h: (0, h, 0, 0)),
        out_shape=jax.ShapeDtypeStruct((3, NA_HEADS, nq, nk), F32),
        scratch_shapes=[pltpu.VMEM((2 * NA_WIN_ROWS, GRID_W, LANES), F32)],
        compiler_params=_cparams("arbitrary"),
        name="na_bias",
    )(rpb_flat)


def _half_masks(value=1.0):
    lane = lax.broadcasted_iota(jnp.int32, (1, LANES), 1)
    lo = jnp.where(lane < LANES // 2, value, 0.0).astype(BF16)
    hi = jnp.where(lane < LANES // 2, 0.0, value).astype(BF16)
    return lo, hi


def _na_body(q_ref, k_ref, v_ref, b_ref, o_ref, *, rows):
    rb = pl.program_id(1)
    s_row = jnp.clip(rb * NA_QROWS - NA_WIN_ROWS // 2, 0, rows - NA_KROWS)
    start = pl.multiple_of(s_row * GRID_W, GRID_W)
    nk = NA_KROWS * GRID_W
    masks = list(zip(_half_masks(NA_HEAD_DIM ** -0.5), _half_masks()))
    for hp in range(NA_HEADS // 2):
        cols = slice(hp * LANES, (hp + 1) * LANES)
        q = q_ref[:, cols]
        kw = k_ref[pl.ds(start, nk), cols]
        vw = v_ref[pl.ds(start, nk), cols]
        acc = jnp.zeros(q.shape, F32)
        for h, (mq, mv) in enumerate(masks):
            s = lax.dot_general(q * mq, kw, _NT, preferred_element_type=F32) + b_ref[2 * hp + h]
            p = jnp.exp(s - jnp.max(s, axis=-1, keepdims=True))
            l = jnp.sum(p, axis=-1, keepdims=True)
            o = jnp.dot(p.astype(BF16), vw * mv, preferred_element_type=F32)
            acc = acc + o * (1.0 / l)
        o_ref[:, cols] = acc.astype(o_ref.dtype)


def _na(proj3, nb, rows):
    b, seq, _ = proj3.shape
    nrb = rows // NA_QROWS
    nq, nk = NA_QROWS * GRID_W, NA_KROWS * GRID_W

    def variant(rb):
        return jnp.where(rb == 0, 0, jnp.where(rb == nrb - 1, 2, 1))

    return pl.pallas_call(
        functools.partial(_na_body, rows=rows),
        grid=(b, nrb),
        in_specs=[pl.BlockSpec((None, nq, NA_WIDTH), lambda bi, rb: (bi, rb, 0)),
                  pl.BlockSpec((None, seq, NA_WIDTH), lambda bi, rb: (bi, 0, 1)),
                  pl.BlockSpec((None, seq, NA_WIDTH), lambda bi, rb: (bi, 0, 2)),
                  pl.BlockSpec((None, NA_HEADS, nq, nk), lambda bi, rb: (variant(rb), 0, 0, 0))],
        out_specs=pl.BlockSpec((None, nq, NA_WIDTH), lambda bi, rb: (bi, rb, 0)),
        out_shape=jax.ShapeDtypeStruct((b, seq, NA_WIDTH), BF16),
        compiler_params=_cparams("arbitrary", "arbitrary"),
        name="na",
    )(proj3, proj3, proj3, nb)


def _t5_thresholds():
    nb = T5_BUCKETS // 2
    me = nb // 2
    span = nb - me
    ratio = T5_MAX_DIST // me
    out = []
    for m in range(1, span):
        n = me
        while n ** span < me ** span * ratio ** m:
            n += 1
        out.append(n)
    return out


def _t5_bias_body(t5_ref, o_ref, *, nq):
    h = pl.program_id(0)
    nb = T5_BUCKETS // 2
    me = nb // 2
    q = lax.broadcasted_iota(jnp.int32, (DIFF_TILE, DIFF_TILE), 0)
    k = lax.broadcasted_iota(jnp.int32, (DIFF_TILE, DIFF_TILE), 1)
    rel = (pl.program_id(1) - (nq - 1)) * DIFF_TILE + k - q
    n = jnp.abs(rel)
    large = jnp.full(n.shape, me, jnp.int32)
    for thr in _t5_thresholds():
        large = large + jnp.where(n >= thr, 1, 0)
    bkt = jnp.where(rel > 0, nb, 0) + jnp.where(n < me, n, large)
    acc = jnp.zeros(bkt.shape, F32)
    for bb in range(T5_BUCKETS):
        acc = jnp.where(bkt == bb, t5_ref[bb * DIFF_HEADS + h], acc)
    o_ref[...] = acc


def _t5_bias(t5_flat, seq):
    nq = seq // DIFF_TILE
    nt = 2 * nq - 1
    return pl.pallas_call(
        functools.partial(_t5_bias_body, nq=nq),
        grid=(DIFF_HEADS, nt),
        in_specs=[pl.BlockSpec(memory_space=pltpu.SMEM)],
        out_specs=pl.BlockSpec((None, None, DIFF_TILE, DIFF_TILE), lambda h, t: (h, t, 0, 0)),
        out_shape=jax.ShapeDtypeStruct((DIFF_HEADS, nt, DIFF_TILE, DIFF_TILE), F32),
        compiler_params=_cparams("arbitrary", "arbitrary"),
        name="t5_bias",
    )(t5_flat)


def _diff_body(q_ref, k_ref, v_ref, dt_ref, lq1_ref, lk1_ref, lq2_ref, lk2_ref, g_ref, o_ref,
               *, nkb, lambda_init):
    qi = pl.program_id(2)
    q = q_ref[...]
    nsub = DIFF_TQ // DIFF_TILE
    lam = (jnp.exp(jnp.sum(lq1_ref[...] * lk1_ref[...], axis=-1, keepdims=True))
           - jnp.exp(jnp.sum(lq2_ref[...] * lk2_ref[...], axis=-1, keepdims=True)) + lambda_init)
    maps = []
    for m in _half_masks(DIFF_HEAD_DIM ** -0.5):
        qm = q * m
        pieces = []
        for kb in range(nkb):
            s = lax.dot_general(qm, k_ref[kb * DIFF_TILE:(kb + 1) * DIFF_TILE, :], _NT,
                                preferred_element_type=F32)
            bias = [dt_ref[nkb - 1 - (nsub * qi + hq) + kb] for hq in range(nsub)]
            pieces.append(s + jnp.concatenate(bias, axis=0))
        s = jnp.concatenate(pieces, axis=1)
        p = jnp.exp(s - jnp.max(s, axis=-1, keepdims=True))
        maps.append((p, jnp.sum(p, axis=-1, keepdims=True)))
    (p1, l1), (p2, l2) = maps
    w = p1 * (1.0 / l1) - p2 * (lam / l2)
    o = jnp.dot(w.astype(BF16), v_ref[...], preferred_element_type=F32)
    o = o * lax.rsqrt(jnp.mean(o * o, axis=-1, keepdims=True) + LN_EPS) * g_ref[...]
    o_ref[...] = (o * (1.0 - lambda_init)).astype(o_ref.dtype)


def _diff(proj3, dt, lq1, lk1, lq2, lk2, subln, lambda_init):
    b, seq, _ = proj3.shape
    nkb = seq // DIFF_TILE
    nt = 2 * nkb - 1
    c0 = 3 * NA_WIDTH // LANES
    vec = lambda n: pl.BlockSpec((1, n), lambda h, bi, qi: (0, 0))
    return pl.pallas_call(
        functools.partial(_diff_body, nkb=nkb, lambda_init=lambda_init),
        grid=(DIFF_HEADS, b, seq // DIFF_TQ),
        in_specs=[pl.BlockSpec((None, DIFF_TQ, LANES), lambda h, bi, qi: (bi, qi, c0 + h)),
                  pl.BlockSpec((None, seq, LANES), lambda h, bi, qi: (bi, 0, c0 + DIFF_HEADS + h)),
                  pl.BlockSpec((None, seq, LANES), lambda h, bi, qi: (bi, 0, c0 + 2 * DIFF_HEADS + h)),
                  pl.BlockSpec((None, nt, DIFF_TILE, DIFF_TILE), lambda h, bi, qi: (h, 0, 0, 0)),
                  vec(DIFF_HEAD_DIM), vec(DIFF_HEAD_DIM), vec(DIFF_HEAD_DIM), vec(DIFF_HEAD_DIM),
                  vec(DIFF_V_DIM)],
        out_specs=pl.BlockSpec((None, DIFF_TQ, LANES), lambda h, bi, qi: (bi, qi, h)),
        out_shape=jax.ShapeDtypeStruct((b, seq, DIFF_WIDTH), BF16),
        compiler_params=_cparams("arbitrary", "arbitrary", "arbitrary"),
        name="diff",
    )(proj3, proj3, proj3, dt, lq1, lk1, lq2, lk2, subln)


def _mid_body(ona_ref, odf_ref, wout_ref, x_ref, ga_ref, scf_ref, shf_ref, g1_ref, b1_ref,
              wq_ref, sk_ref, x1_ref, h2_ref, st_ref):
    mix = (jnp.dot(ona_ref[...], wout_ref[0:NA_WIDTH, :], preferred_element_type=F32)
           + jnp.dot(odf_ref[...], wout_ref[NA_WIDTH:, :], preferred_element_type=F32))
    y = DN_ALPHA * x_ref[...] + (1.0 + ga_ref[...]) * mix
    x1 = _ln(y) * g1_ref[...] + b1_ref[...]
    x1_ref[...] = x1
    h2 = (_ln(x1) * (1.0 + scf_ref[...]) + shf_ref[...]).astype(BF16)
    h2_ref[...] = h2
    qp = jnp.dot(h2, wq_ref[...], preferred_element_type=F32).astype(BF16)
    for hp in range(2 * PEER_HEADS):
        st_ref[hp] = lax.dot_general(sk_ref[hp % 2], qp[:, hp * PEER_NKEYS:(hp + 1) * PEER_NKEYS],
                                     _NT, preferred_element_type=F32)


def _mid(o_na, o_df, w_out, xf, mod4, ln_g, ln_b, w_query, sub_keys, seq):
    t, d = xf.shape
    tpb = seq // MID_TM
    nqk = w_query.shape[1]
    modspec = lambda j: pl.BlockSpec((None, None, 1, d), lambda i: (i // tpb, j, 0, 0))
    full = lambda *s: pl.BlockSpec(s, lambda i: (0,) * len(s))
    return pl.pallas_call(
        _mid_body,
        grid=(t // MID_TM,),
        in_specs=[pl.BlockSpec((MID_TM, NA_WIDTH), lambda i: (i, 0)),
                  pl.BlockSpec((MID_TM, DIFF_WIDTH), lambda i: (i, 0)),
                  full(d, d),
                  pl.BlockSpec((MID_TM, d), lambda i: (i, 0)),
                  modspec(2), modspec(4), modspec(3),
                  full(1, d), full(1, d),
                  full(d, nqk),
                  full(2, PEER_NKEYS, PEER_NKEYS)],
        out_specs=[pl.BlockSpec((MID_TM, d), lambda i: (i, 0)),
                   pl.BlockSpec((MID_TM, d), lambda i: (i, 0)),
                   pl.BlockSpec((2 * PEER_HEADS, PEER_NKEYS, MID_TM), lambda i: (0, 0, i))],
        out_shape=[jax.ShapeDtypeStruct((t, d), F32),
                   jax.ShapeDtypeStruct((t, d), BF16),
                   jax.ShapeDtypeStruct((2 * PEER_HEADS, PEER_NKEYS, t), F32)],
        compiler_params=_cparams("arbitrary"),
        name="mid",
    )(o_na, o_df, w_out, xf, mod4, mod4, mod4, ln_g, ln_b, w_query, sub_keys)


def _extract_top(work, order, tie_rule, out_ref=None, tok=None):
    rank = jnp.full(work.shape, float(PEER_TOPK), F32)
    big = float(PEER_N)
    for r in range(PEER_TOPK):
        m = jnp.max(work, axis=0, keepdims=True)
        sel = work == m
        if tie_rule:
            first = jnp.min(jnp.where(sel, order, big), axis=0, keepdims=True)
            sel = order == first
        rank = jnp.where(sel, float(r), rank)
        work = jnp.where(sel, -jnp.inf, work)
        if out_ref is not None:
            out_ref[r:r + 1, tok] = m
    return rank


def _column_max(x):
    slabs = [x[i:i + 8] for i in range(0, x.shape[0], 8)]
    while len(slabs) > 1:
        slabs = [jnp.maximum(slabs[i], slabs[i + 1]) if i + 1 < len(slabs) else slabs[i]
                 for i in range(0, len(slabs), 2)]
    return jnp.max(slabs[0], axis=0, keepdims=True)


def _pair_cells(op, u1, u2):
    g = [op(u1[0:1], u2[0:8]), op(u1[0:1], u2[8:16])]
    for a in range(1, 8):
        g.append(op(u1[a:a + 1], u2[0:8]))
    g.append(op(u1[8:16], u2[0:1]))
    return jnp.concatenate(g, axis=0)


def _topk_body(st_ref, lc_ref, f1_ref, r2_ref, f2_ref, t_ref, rk_ref, l_ref, z_ref):
    tl = st_ref.shape[-1]
    chunks = [slice(c * LANES, (c + 1) * LANES) for c in range(tl // LANES)]
    key_id = lax.broadcasted_iota(jnp.int32, (PEER_NKEYS, LANES), 0).astype(F32)

    def half_ranks(tie_rule):
        for tok in chunks:
            for p in range(2):
                rk_ref[p, :, tok] = _extract_top(st_ref[p, :, tok], key_id, tie_rule,
                                                 t_ref.at[p], tok)

    for tok in chunks:
        work = [st_ref[p, :, tok] for p in range(2)]
        for p in range(2):
            rk_ref[p, :, tok] = jnp.full((PEER_NKEYS, LANES), float(PEER_TOPK), F32)
        for r in range(PEER_TOPK):
            for p in range(2):
                m = _column_max(work[p])
                sel = work[p] == m
                rk_ref[p, :, tok] = jnp.where(sel, float(r), rk_ref[p, :, tok])
                work[p] = jnp.where(sel, -jnp.inf, work[p])
                t_ref[p, r:r + 1, tok] = m
    taken = jnp.where(rk_ref[...] < float(PEER_TOPK), 1.0, 0.0)
    most = jnp.max(jnp.sum(taken, axis=1, keepdims=True))

    @pl.when(most > float(PEER_TOPK))
    def _():
        half_ranks(True)

    wide = 2 * LANES
    row = lax.broadcasted_iota(jnp.int32, (80, wide), 0)
    mid = row - 16
    cell_id = jnp.where(row < 16, row,
                        jnp.where(row < 72,
                                  (jnp.right_shift(mid, 3) + 1) * 16 + jnp.bitwise_and(mid, 7),
                                  (row - 64) * 16)).astype(F32)
    for c0 in range(0, tl, wide):
        tok2 = slice(c0, c0 + wide)
        t1 = t_ref[0, :, tok2]
        t2 = t_ref[1, :, tok2]
        work = _pair_cells(jnp.add, t1, t2)
        for _ in range(PEER_TOPK):
            m = jnp.max(work, axis=0, keepdims=True)
            first = jnp.min(jnp.where(work == m, cell_id, float(PEER_N)), axis=0, keepdims=True)
            work = jnp.where(cell_id == first, -jnp.inf, work)
        chosen = jnp.where(work == -jnp.inf, 1.0, 0.0)
        e1 = jnp.exp(t1 - t1[0:1])
        e2 = jnp.exp(t2 - t2[0:1])
        z = jnp.sum(chosen * _pair_cells(jnp.multiply, e1, e2), axis=0, keepdims=True)
        z_ref[0:1, tok2] = 1.0 / z
        l_ref[0:1, tok2] = jnp.sum(chosen[0:16], axis=0, keepdims=True)
        for a in range(1, 8):
            l_ref[a:a + 1, tok2] = jnp.sum(chosen[8 + 8 * a:16 + 8 * a], axis=0, keepdims=True)
        l_ref[8:16, tok2] = chosen[72:80]
    for tok in chunks:
        rank1 = rk_ref[0, :, tok]
        rank2 = rk_ref[1, :, tok]
        lcnt = l_ref[:, tok]
        lc = jnp.zeros((PEER_NKEYS, LANES), F32)
        for a in range(PEER_TOPK):
            lc = jnp.where(rank1 == float(a), lcnt[a:a + 1], lc)
        lc_ref[:, tok] = lc
        f1_ref[:, tok] = jnp.exp(st_ref[0, :, tok] - t_ref[0, 0:1, tok]) * z_ref[0:1, tok]
        r2_ref[:, tok] = rank2
        f2_ref[:, tok] = jnp.where(rank2 < float(PEER_TOPK),
                                   jnp.exp(st_ref[1, :, tok] - t_ref[1, 0:1, tok]), 0.0)


def _topk(st):
    _, nk, t = st.shape
    tl = min(TOPK_TL, t)
    spec = pl.BlockSpec((None, nk, tl), lambda i, h: (h, 0, i))
    shp = jax.ShapeDtypeStruct((PEER_HEADS, nk, t), F32)
    return pl.pallas_call(
        _topk_body,
        grid=(t // tl, PEER_HEADS),
        in_specs=[pl.BlockSpec((2, nk, tl), lambda i, h: (h, 0, i))],
        out_specs=[spec, spec, spec, spec],
        out_shape=[shp, shp, shp, shp],
        scratch_shapes=[pltpu.VMEM((2, PEER_TOPK, tl), F32), pltpu.VMEM((2, nk, tl), F32),
                        pltpu.VMEM((PEER_TOPK, tl), F32), pltpu.VMEM((8, tl), F32)],
        compiler_params=_cparams("arbitrary", "arbitrary"),
        name="topk",
    )(st)


def _peer_body(h2_ref, ed_ref, eut_ref, lc_ref, f1_ref, r2_ref, f2_ref, x1_ref, gf_ref,
               g2_ref, b2_ref, o_ref, acc_ref, *, ni):
    et = pl.program_id(1)

    @pl.when(et == 0)
    def _():
        acc_ref[...] = jnp.zeros(acc_ref.shape, F32)

    tm = h2_ref.shape[0]
    sub = BF16_SUBLANES
    kgroup = 2 * sub
    at = lax.dot_general(ed_ref[...], h2_ref[...], _NT, preferred_element_type=F32)
    cols = []
    for c in range(tm // LANES):
        tok = slice(c * LANES, (c + 1) * LANES)
        w_rows = {}
        for k0 in range(0, PEER_NKEYS, kgroup):
            subs = [slice(k0 + j * sub, k0 + (j + 1) * sub) for j in range(kgroup // sub)]
            rank = [[r2_ref[h, rk, tok].astype(BF16) for rk in subs] for h in range(PEER_HEADS)]
            val = [[f2_ref[h, rk, tok].astype(BF16) for rk in subs] for h in range(PEER_HEADS)]
            for ii in range(ni):
                g = [None] * len(subs)
                for h in range(PEER_HEADS):
                    cnt = jnp.broadcast_to(lc_ref[h, ii:ii + 1, tok], (sub, LANES)).astype(BF16)
                    fac = jnp.broadcast_to(f1_ref[h, ii:ii + 1, tok], (sub, LANES)).astype(BF16)
                    for j in range(len(subs)):
                        term = jnp.where(rank[h][j] < cnt, val[h][j], 0) * fac
                        g[j] = term if g[j] is None else g[j] + term
                for j, rk in enumerate(subs):
                    r0 = ii * PEER_NKEYS + rk.start
                    a = at[r0:r0 + sub, tok]
                    act = 0.5 * a * (1.0 + lax.erf(a * (2.0 ** -0.5)))
                    w_rows[r0] = g[j] * act.astype(BF16)
        cols.append(jnp.concatenate([w_rows[r0] for r0 in sorted(w_rows)], axis=0))
    wt = jnp.concatenate(cols, axis=1)
    acc_ref[...] += jnp.dot(eut_ref[...], wt, preferred_element_type=F32)

    @pl.when(et == pl.num_programs(1) - 1)
    def _():
        f = acc_ref[...].T
        y = DN_ALPHA * x1_ref[...] + (1.0 + gf_ref[...]) * f
        o_ref[...] = _ln(y) * g2_ref[...] + b2_ref[...]


def _peer(h2, ed, eut, lc, f1, r2, f2, x1, mod4, ln_g, ln_b, seq):
    t, d = h2.shape
    tm = min(PEER_TM, t)
    te = PEER_TE
    ni = te // PEER_NKEYS
    ne = PEER_N // te
    tpb = seq // tm
    lc4 = lc.reshape(PEER_HEADS, ne, ni, t)
    f14 = f1.reshape(PEER_HEADS, ne, ni, t)
    rowspec = pl.BlockSpec((PEER_HEADS, None, ni, tm), lambda i, e: (0, e, 0, i))
    colspec = pl.BlockSpec((PEER_HEADS, PEER_NKEYS, tm), lambda i, e: (0, 0, i))
    full = lambda *s: pl.BlockSpec(s, lambda i, e: (0,) * len(s))
    return pl.pallas_call(
        functools.partial(_peer_body, ni=ni),
        grid=(t // tm, ne),
        in_specs=[pl.BlockSpec((tm, d), lambda i, e: (i, 0)),
                  pl.BlockSpec((te, d), lambda i, e: (e, 0)),
                  pl.BlockSpec((d, te), lambda i, e: (0, e)),
                  rowspec, rowspec, colspec, colspec,
                  pl.BlockSpec((tm, d), lambda i, e: (i, 0)),
                  pl.BlockSpec((None, None, 1, d), lambda i, e: (i // tpb, 5, 0, 0)),
                  full(1, d), full(1, d)],
        out_specs=pl.BlockSpec((tm, d), lambda i, e: (i, 0)),
        out_shape=jax.ShapeDtypeStruct((t, d), F32),
        scratch_shapes=[pltpu.VMEM((d, tm), F32)],
        compiler_params=_cparams("arbitrary", "arbitrary"),
        name="peer",
    )(h2, ed, eut, lc4, f14, r2, f2, x1, mod4, ln_g, ln_b)


def kernel(x, c, w_ada, b_ada, w_in, w_out, na_rpb, t5_bias, lambda_q1, lambda_k1, lambda_q2,
           lambda_k2, diff_subln, ln1_g, ln1_b, w_query, sub_keys, expert_down, expert_up,
           ln2_g, ln2_b):
    b, seq, d = x.shape
    t = b * seq
    rows = seq // GRID_W
    assert d == D_MODEL and w_ada.shape[0] == DEPTH == 1
    assert rows % NA_QROWS == 0 and rows >= NA_KROWS and seq % DIFF_TQ == 0
    l = 0
    lambda_init = 0.8 - 0.6 * math.exp(-0.3 * l)
    xf = x.reshape(t, d)

    mod4 = _ada(c, w_ada[l], b_ada[l][None, :]).reshape(b, 6, 1, d)
    proj3 = _proj(xf, mod4, w_in[l].astype(BF16), seq).reshape(b, seq, IN_WIDTH)

    nb = _na_bias(na_rpb[l].reshape(-1), rows)
    o_na = _na(proj3, nb, rows).reshape(t, NA_WIDTH)

    dt = _t5_bias(t5_bias.reshape(-1), seq)
    o_df = _diff(proj3, dt, lambda_q1[l][None, :], lambda_k1[l][None, :], lambda_q2[l][None, :],
                 lambda_k2[l][None, :], diff_subln[l][None, :], lambda_init).reshape(t, DIFF_WIDTH)

    x1, h2, st = _mid(o_na, o_df, w_out[l].astype(BF16), xf, mod4, ln1_g[l][None, :],
                      ln1_b[l][None, :], w_query[l].astype(BF16), sub_keys[l].astype(BF16), seq)
    lc, f1, r2, f2 = _topk(st)
    out = _peer(h2, expert_down[l].astype(BF16), expert_up[l].astype(BF16).T, lc, f1, r2, f2,
                x1, mod4, ln2_g[l][None, :], ln2_b[l][None, :], seq)
    return out.reshape(b, seq, d)
```

```python
import functools
import math

import jax
import jax.numpy as jnp
from jax import lax
from jax.experimental import pallas as pl
from jax.experimental.pallas import tpu as pltpu

F32 = jnp.float32
BF16 = jnp.bfloat16

D_MODEL = 1024
DEPTH = 1
GRID_W = 64
NA_HEADS = 8
NA_HEAD_DIM = 64
NA_WIDTH = NA_HEADS * NA_HEAD_DIM
NA_WIN_ROWS = 8
NA_WIN_COLS = 16
DIFF_HEADS = 4
DIFF_HEAD_DIM = 64
DIFF_V_DIM = 2 * DIFF_HEAD_DIM
DIFF_WIDTH = DIFF_HEADS * DIFF_V_DIM
IN_WIDTH = 3 * NA_WIDTH + 3 * DIFF_WIDTH
T5_BUCKETS = 32
T5_MAX_DIST = 128
PEER_HEADS = 8
PEER_NKEYS = 128
PEER_N = PEER_NKEYS * PEER_NKEYS
PEER_TOPK = 16
LN_EPS = 1e-5
DN_ALPHA = (2.0 * DEPTH) ** 0.25
NEG_BIG = -1e30
LANES = 128
BF16_SUBLANES = 16

PROJ_TM = 512
NA_QROWS = 4
NA_KROWS = 12
DIFF_TQ = 256
DIFF_TILE = 256
MID_TM = 512
TOPK_TL = 1024
FAST_CHAINS = 4
PEER_TM = 512
PEER_TE = 2048
VMEM_LIMIT = 56 * 1024 * 1024

_NT = (((1,), (1,)), ((), ()))


def _cparams(*sem):
    return pltpu.CompilerParams(dimension_semantics=sem, vmem_limit_bytes=VMEM_LIMIT)


def _ln(x):
    mu = jnp.mean(x, axis=-1, keepdims=True)
    xc = x - mu
    var = jnp.mean(xc * xc, axis=-1, keepdims=True)
    return xc * lax.rsqrt(var + LN_EPS)


def _ada_body(c_ref, w_ref, b_ref, o_ref):
    c = c_ref[...]
    s = c * jax.nn.sigmoid(c)
    o_ref[...] = jnp.dot(s, w_ref[...], preferred_element_type=F32) + b_ref[...]


def _ada(c, w, bias):
    b, d = c.shape
    n = w.shape[1]
    return pl.pallas_call(
        _ada_body,
        grid=(n // d,),
        in_specs=[pl.BlockSpec((b, d), lambda j: (0, 0)),
                  pl.BlockSpec((d, d), lambda j: (0, j)),
                  pl.BlockSpec((1, d), lambda j: (0, j))],
        out_specs=pl.BlockSpec((b, d), lambda j: (0, j)),
        out_shape=jax.ShapeDtypeStruct((b, n), F32),
        compiler_params=_cparams("arbitrary"),
        name="ada",
    )(c, w, bias)


def _proj_body(x_ref, sc_ref, sh_ref, w_ref, o_ref):
    h = _ln(x_ref[...]) * (1.0 + sc_ref[...]) + sh_ref[...]
    o_ref[...] = jnp.dot(h.astype(BF16), w_ref[...], preferred_element_type=F32).astype(o_ref.dtype)


def _proj(xf, mod4, w, seq):
    t, d = xf.shape
    n = w.shape[1]
    tpb = seq // PROJ_TM
    return pl.pallas_call(
        _proj_body,
        grid=(t // PROJ_TM,),
        in_specs=[pl.BlockSpec((PROJ_TM, d), lambda i: (i, 0)),
                  pl.BlockSpec((None, None, 1, d), lambda i: (i // tpb, 1, 0, 0)),
                  pl.BlockSpec((None, None, 1, d), lambda i: (i // tpb, 0, 0, 0)),
                  pl.BlockSpec((d, n), lambda i: (0, 0))],
        out_specs=pl.BlockSpec((PROJ_TM, n), lambda i: (i, 0)),
        out_shape=jax.ShapeDtypeStruct((t, n), BF16),
        compiler_params=_cparams("arbitrary"),
        name="proj",
    )(xf, mod4, mod4, w)


def _na_row_offsets(rows):
    kr = min(NA_WIN_ROWS, rows)
    nrb = rows // NA_QROWS
    pats = []
    for rb in range(nrb):
        r0 = rb * NA_QROWS
        s = min(max(r0 - NA_WIN_ROWS // 2, 0), rows - NA_KROWS)
        pat = []
        for qr in range(NA_QROWS):
            r = r0 + qr
            rs = min(max(r - kr // 2, 0), rows - kr)
            pat.append(tuple((s + i) - r + (NA_WIN_ROWS - 1) if rs <= s + i < rs + kr else 15
                             for i in range(NA_KROWS)))
        pats.append(tuple(pat))
    variants = [pats[0], pats[1], pats[-1]]
    assert all(p == variants[1] for p in pats[1:-1])
    return variants


def _na_bias_body(rpb_ref, o_ref, tile_ref, *, variants):
    h = pl.program_id(0)
    nro = 2 * NA_WIN_ROWS - 1
    nco = 2 * NA_WIN_COLS - 1
    q = lax.broadcasted_iota(jnp.int32, (GRID_W, LANES), 0)
    kk = lax.broadcasted_iota(jnp.int32, (GRID_W, LANES), 1)
    k = jnp.bitwise_and(kk, GRID_W - 1)
    col_start = jnp.clip(q - NA_WIN_COLS // 2, 0, GRID_W - NA_WIN_COLS)
    valid = jnp.logical_and(k >= col_start, k < col_start + NA_WIN_COLS)
    co = jnp.clip(k - q, -(NA_WIN_COLS - 1), NA_WIN_COLS - 1) + (NA_WIN_COLS - 1)
    neg = jnp.full((GRID_W, LANES), NEG_BIG, F32)
    for ro in range(nro):
        acc = neg
        for cc in range(nco):
            acc = jnp.where(co == cc, rpb_ref[(h * nro + ro) * nco + cc], acc)
        tile_ref[ro] = jnp.where(valid, acc, NEG_BIG)
    tile_ref[nro] = neg
    left = kk < GRID_W
    for v, pat in enumerate(variants):
        for qr in range(NA_QROWS):
            for kp in range(NA_KROWS // 2):
                a, b = pat[qr][2 * kp], pat[qr][2 * kp + 1]
                o_ref[v, qr * GRID_W:(qr + 1) * GRID_W, kp * LANES:(kp + 1) * LANES] = (
                    jnp.where(left, tile_ref[a], tile_ref[b]))


def _na_bias(rpb_flat, rows):
    variants = _na_row_offsets(rows)
    nq, nk = NA_QROWS * GRID_W, NA_KROWS * GRID_W
    return pl.pallas_call(
        functools.partial(_na_bias_body, variants=variants),
        grid=(NA_HEADS,),
        in_specs=[pl.BlockSpec(memory_space=pltpu.SMEM)],
        out_specs=pl.BlockSpec((3, None, nq, nk), lambda h: (0, h, 0, 0)),
        out_shape=jax.ShapeDtypeStruct((3, NA_HEADS, nq, nk), F32),
        scratch_shapes=[pltpu.VMEM((2 * NA_WIN_ROWS, GRID_W, LANES), F32)],
        compiler_params=_cparams("arbitrary"),
        name="na_bias",
    )(rpb_flat)


def _half_masks(value=1.0):
    lane = lax.broadcasted_iota(jnp.int32, (1, LANES), 1)
    lo = jnp.where(lane < LANES // 2, value, 0.0).astype(BF16)
    hi = jnp.where(lane < LANES // 2, 0.0, value).astype(BF16)
    return lo, hi


def _na_body(q_ref, k_ref, v_ref, b_ref, o_ref, *, rows):
    rb = pl.program_id(0)
    s_row = jnp.clip(rb * NA_QROWS - NA_WIN_ROWS // 2, 0, rows - NA_KROWS)
    start = pl.multiple_of(s_row * GRID_W, GRID_W)
    nk = NA_KROWS * GRID_W
    masks = list(zip(_half_masks(NA_HEAD_DIM ** -0.5), _half_masks()))
    for hp in range(NA_HEADS // 2):
        cols = slice(hp * LANES, (hp + 1) * LANES)
        q = q_ref[:, cols]
        kw = k_ref[pl.ds(start, nk), cols]
        vw = v_ref[pl.ds(start, nk), cols]
        acc = jnp.zeros(q.shape, F32)
        for h, (mq, mv) in enumerate(masks):
            s = lax.dot_general(q * mq, kw, _NT, preferred_element_type=F32) + b_ref[2 * hp + h]
            p = jnp.exp(s - jnp.max(s, axis=-1, keepdims=True))
            l = jnp.sum(p, axis=-1, keepdims=True)
            o = jnp.dot(p.astype(BF16), vw * mv, preferred_element_type=F32)
            acc = acc + o * (1.0 / l)
        o_ref[:, cols] = acc.astype(o_ref.dtype)


def _na(proj3, nb, rows):
    b, seq, _ = proj3.shape
    nrb = rows // NA_QROWS
    nq, nk = NA_QROWS * GRID_W, NA_KROWS * GRID_W

    def variant(rb):
        return jnp.where(rb == 0, 0, jnp.where(rb == nrb - 1, 2, 1))

    return pl.pallas_call(
        functools.partial(_na_body, rows=rows),
        grid=(nrb, b),
        in_specs=[pl.BlockSpec((None, nq, NA_WIDTH), lambda rb, bi: (bi, rb, 0)),
                  pl.BlockSpec((None, seq, NA_WIDTH), lambda rb, bi: (bi, 0, 1)),
                  pl.BlockSpec((None, seq, NA_WIDTH), lambda rb, bi: (bi, 0, 2)),
                  pl.BlockSpec((None, NA_HEADS, nq, nk), lambda rb, bi: (variant(rb), 0, 0, 0))],
        out_specs=pl.BlockSpec((None, nq, NA_WIDTH), lambda rb, bi: (bi, rb, 0)),
        out_shape=jax.ShapeDtypeStruct((b, seq, NA_WIDTH), BF16),
        compiler_params=_cparams("arbitrary", "arbitrary"),
        name="na",
    )(proj3, proj3, proj3, nb)


def _t5_thresholds():
    nb = T5_BUCKETS // 2
    me = nb // 2
    span = nb - me
    ratio = T5_MAX_DIST // me
    out = []
    for m in range(1, span):
        n = me
        while n ** span < me ** span * ratio ** m:
            n += 1
        out.append(n)
    return out


def _t5_bias_body(t5_ref, o_ref, *, nq):
    h = pl.program_id(0)
    nb = T5_BUCKETS // 2
    me = nb // 2
    q = lax.broadcasted_iota(jnp.int32, (DIFF_TILE, DIFF_TILE), 0)
    k = lax.broadcasted_iota(jnp.int32, (DIFF_TILE, DIFF_TILE), 1)
    rel = (pl.program_id(1) - (nq - 1)) * DIFF_TILE + k - q
    n = jnp.abs(rel)
    large = jnp.full(n.shape, me, jnp.int32)
    for thr in _t5_thresholds():
        large = large + jnp.where(n >= thr, 1, 0)
    bkt = jnp.where(rel > 0, nb, 0) + jnp.where(n < me, n, large)
    acc = jnp.zeros(bkt.shape, F32)
    for bb in range(T5_BUCKETS):
        acc = jnp.where(bkt == bb, t5_ref[bb * DIFF_HEADS + h], acc)
    o_ref[...] = acc


def _t5_bias(t5_flat, seq):
    nq = seq // DIFF_TILE
    nt = 2 * nq - 1
    return pl.pallas_call(
        functools.partial(_t5_bias_body, nq=nq),
        grid=(DIFF_HEADS, nt),
        in_specs=[pl.BlockSpec(memory_space=pltpu.SMEM)],
        out_specs=pl.BlockSpec((None, None, DIFF_TILE, DIFF_TILE), lambda h, t: (h, t, 0, 0)),
        out_shape=jax.ShapeDtypeStruct((DIFF_HEADS, nt, DIFF_TILE, DIFF_TILE), F32),
        compiler_params=_cparams("arbitrary", "arbitrary"),
        name="t5_bias",
    )(t5_flat)


def _diff_body(q_ref, k_ref, v_ref, dt_ref, lq1_ref, lk1_ref, lq2_ref, lk2_ref, g_ref, o_ref,
               *, nkb, lambda_init):
    qi = pl.program_id(2)
    q = q_ref[...]
    nsub = DIFF_TQ // DIFF_TILE
    lam = (jnp.exp(jnp.sum(lq1_ref[...] * lk1_ref[...], axis=-1, keepdims=True))
           - jnp.exp(jnp.sum(lq2_ref[...] * lk2_ref[...], axis=-1, keepdims=True)) + lambda_init)
    maps = []
    for m in _half_masks(DIFF_HEAD_DIM ** -0.5):
        qm = q * m
        pieces = []
        for kb in range(nkb):
            s = lax.dot_general(qm, k_ref[kb * DIFF_TILE:(kb + 1) * DIFF_TILE, :], _NT,
                                preferred_element_type=F32)
            bias = [dt_ref[nkb - 1 - (nsub * qi + hq) + kb] for hq in range(nsub)]
            pieces.append(s + jnp.concatenate(bias, axis=0))
        s = jnp.concatenate(pieces, axis=1)
        p = jnp.exp(s - jnp.max(s, axis=-1, keepdims=True))
        maps.append((p, jnp.sum(p, axis=-1, keepdims=True)))
    (p1, l1), (p2, l2) = maps
    w = p1 * (1.0 / l1) - p2 * (lam / l2)
    o = jnp.dot(w.astype(BF16), v_ref[...], preferred_element_type=F32)
    o = o * lax.rsqrt(jnp.mean(o * o, axis=-1, keepdims=True) + LN_EPS) * g_ref[...]
    o_ref[...] = (o * (1.0 - lambda_init)).astype(o_ref.dtype)


def _diff(proj3, dt, lq1, lk1, lq2, lk2, subln, lambda_init):
    b, seq, _ = proj3.shape
    nkb = seq // DIFF_TILE
    nt = 2 * nkb - 1
    c0 = 3 * NA_WIDTH // LANES
    vec = lambda n: pl.BlockSpec((1, n), lambda h, bi, qi: (0, 0))
    return pl.pallas_call(
        functools.partial(_diff_body, nkb=nkb, lambda_init=lambda_init),
        grid=(DIFF_HEADS, b, seq // DIFF_TQ),
        in_specs=[pl.BlockSpec((None, DIFF_TQ, LANES), lambda h, bi, qi: (bi, qi, c0 + h)),
                  pl.BlockSpec((None, seq, LANES), lambda h, bi, qi: (bi, 0, c0 + DIFF_HEADS + h)),
                  pl.BlockSpec((None, seq, LANES), lambda h, bi, qi: (bi, 0, c0 + 2 * DIFF_HEADS + h)),
                  pl.BlockSpec((None, nt, DIFF_TILE, DIFF_TILE), lambda h, bi, qi: (h, 0, 0, 0)),
                  vec(DIFF_HEAD_DIM), vec(DIFF_HEAD_DIM), vec(DIFF_HEAD_DIM), vec(DIFF_HEAD_DIM),
                  vec(DIFF_V_DIM)],
        out_specs=pl.BlockSpec((None, DIFF_TQ, LANES), lambda h, bi, qi: (bi, qi, h)),
        out_shape=jax.ShapeDtypeStruct((b, seq, DIFF_WIDTH), BF16),
        compiler_params=_cparams("arbitrary", "arbitrary", "arbitrary"),
        name="diff",
    )(proj3, proj3, proj3, dt, lq1, lk1, lq2, lk2, subln)


def _mid_body(ona_ref, odf_ref, wout_ref, x_ref, ga_ref, scf_ref, shf_ref, g1_ref, b1_ref,
              wq_ref, sk_ref, x1_ref, h2_ref, st_ref):
    mix = (jnp.dot(ona_ref[...], wout_ref[0:NA_WIDTH, :], preferred_element_type=F32)
           + jnp.dot(odf_ref[...], wout_ref[NA_WIDTH:, :], preferred_element_type=F32))
    y = DN_ALPHA * x_ref[...] + (1.0 + ga_ref[...]) * mix
    x1 = _ln(y) * g1_ref[...] + b1_ref[...]
    x1_ref[...] = x1
    h2 = (_ln(x1) * (1.0 + scf_ref[...]) + shf_ref[...]).astype(BF16)
    h2_ref[...] = h2
    qp = jnp.dot(h2, wq_ref[...], preferred_element_type=F32).astype(BF16)
    for hp in range(2 * PEER_HEADS):
        st_ref[hp] = lax.dot_general(sk_ref[hp % 2], qp[:, hp * PEER_NKEYS:(hp + 1) * PEER_NKEYS],
                                     _NT, preferred_element_type=F32)


def _mid(o_na, o_df, w_out, xf, mod4, ln_g, ln_b, w_query, sub_keys, seq):
    t, d = xf.shape
    tpb = seq // MID_TM
    nqk = w_query.shape[1]
    modspec = lambda j: pl.BlockSpec((None, None, 1, d), lambda i: (i // tpb, j, 0, 0))
    full = lambda *s: pl.BlockSpec(s, lambda i: (0,) * len(s))
    return pl.pallas_call(
        _mid_body,
        grid=(t // MID_TM,),
        in_specs=[pl.BlockSpec((MID_TM, NA_WIDTH), lambda i: (i, 0)),
                  pl.BlockSpec((MID_TM, DIFF_WIDTH), lambda i: (i, 0)),
                  full(d, d),
                  pl.BlockSpec((MID_TM, d), lambda i: (i, 0)),
                  modspec(2), modspec(4), modspec(3),
                  full(1, d), full(1, d),
                  full(d, nqk),
                  full(2, PEER_NKEYS, PEER_NKEYS)],
        out_specs=[pl.BlockSpec((MID_TM, d), lambda i: (i, 0)),
                   pl.BlockSpec((MID_TM, d), lambda i: (i, 0)),
                   pl.BlockSpec((2 * PEER_HEADS, PEER_NKEYS, MID_TM), lambda i: (0, 0, i))],
        out_shape=[jax.ShapeDtypeStruct((t, d), F32),
                   jax.ShapeDtypeStruct((t, d), BF16),
                   jax.ShapeDtypeStruct((2 * PEER_HEADS, PEER_NKEYS, t), F32)],
        compiler_params=_cparams("arbitrary"),
        name="mid",
    )(o_na, o_df, w_out, xf, mod4, mod4, mod4, ln_g, ln_b, w_query, sub_keys)


def _extract_top(work, order, tie_rule, out_ref=None, tok=None):
    rank = jnp.full(work.shape, float(PEER_TOPK), F32)
    big = float(PEER_N)
    for r in range(PEER_TOPK):
        m = jnp.max(work, axis=0, keepdims=True)
        sel = work == m
        if tie_rule:
            first = jnp.min(jnp.where(sel, order, big), axis=0, keepdims=True)
            sel = order == first
        rank = jnp.where(sel, float(r), rank)
        work = jnp.where(sel, -jnp.inf, work)
        if out_ref is not None:
            out_ref[r:r + 1, tok] = m
    return rank


def _column_max(x):
    slabs = [x[i:i + 8] for i in range(0, x.shape[0], 8)]
    while len(slabs) > 1:
        slabs = [jnp.maximum(slabs[i], slabs[i + 1]) if i + 1 < len(slabs) else slabs[i]
                 for i in range(0, len(slabs), 2)]
    return jnp.max(slabs[0], axis=0, keepdims=True)


def _pair_cells(op, u1, u2):
    g = [op(u1[0:1], u2[0:8]), op(u1[0:1], u2[8:16])]
    for a in range(1, 8):
        g.append(op(u1[a:a + 1], u2[0:8]))
    g.append(op(u1[8:16], u2[0:1]))
    return jnp.concatenate(g, axis=0)


def _topk_body(st_ref, lc_ref, f1_ref, r2_ref, f2_ref, t_ref, rk_ref, l_ref, z_ref, w2_ref):
    tl = st_ref.shape[-1]
    chunks = [slice(c * LANES, (c + 1) * LANES) for c in range(tl // LANES)]
    key_id = lax.broadcasted_iota(jnp.int32, (PEER_NKEYS, LANES), 0).astype(F32)

    def half_ranks(tie_rule):
        for tok in chunks:
            for p in range(2):
                rk_ref[p, :, tok] = _extract_top(st_ref[p, :, tok], key_id, tie_rule,
                                                 t_ref.at[p], tok)

    jobs = [(p, tok) for tok in chunks for p in range(2)]
    for j0 in range(0, len(jobs), FAST_CHAINS):
        group = jobs[j0:j0 + FAST_CHAINS]
        work = [st_ref[p, :, tok] for p, tok in group]
        for p, tok in group:
            rk_ref[p, :, tok] = jnp.full((PEER_NKEYS, LANES), float(PEER_TOPK), F32)
        for r in range(PEER_TOPK):
            for i, (p, tok) in enumerate(group):
                m = _column_max(work[i])
                sel = work[i] == m
                rk_ref[p, :, tok] = jnp.where(sel, float(r), rk_ref[p, :, tok])
                work[i] = jnp.where(sel, -jnp.inf, work[i])
                t_ref[p, r:r + 1, tok] = m
    taken = jnp.where(rk_ref[...] < float(PEER_TOPK), 1.0, 0.0)
    most = jnp.max(jnp.sum(taken, axis=1, keepdims=True))

    @pl.when(most > float(PEER_TOPK))
    def _():
        half_ranks(True)

    wide = 4 * LANES
    row = lax.broadcasted_iota(jnp.int32, (80, wide), 0)
    mid = row - 16
    cell_id = jnp.where(row < 16, row,
                        jnp.where(row < 72,
                                  (jnp.right_shift(mid, 3) + 1) * 16 + jnp.bitwise_and(mid, 7),
                                  (row - 64) * 16)).astype(F32)
    for c0 in range(0, tl, wide):
        tok2 = slice(c0, c0 + wide)
        t1 = t_ref[0, :, tok2]
        t2 = t_ref[1, :, tok2]
        work = _pair_cells(jnp.add, t1, t2)
        for _ in range(PEER_TOPK):
            work = jnp.where(work == _column_max(work), -jnp.inf, work)
        w2_ref[:, tok2] = work
        taken = jnp.sum(jnp.where(work == -jnp.inf, 1.0, 0.0), axis=0, keepdims=True)

        @pl.when(jnp.max(taken) > float(PEER_TOPK))
        def _():
            again = _pair_cells(jnp.add, t_ref[0, :, tok2], t_ref[1, :, tok2])
            for _ in range(PEER_TOPK):
                m = jnp.max(again, axis=0, keepdims=True)
                first = jnp.min(jnp.where(again == m, cell_id, float(PEER_N)), axis=0,
                                keepdims=True)
                again = jnp.where(cell_id == first, -jnp.inf, again)
            w2_ref[:, tok2] = again

        chosen = jnp.where(w2_ref[:, tok2] == -jnp.inf, 1.0, 0.0)
        e1 = jnp.exp(t1 - t1[0:1])
        e2 = jnp.exp(t2 - t2[0:1])
        z = jnp.sum(chosen * _pair_cells(jnp.multiply, e1, e2), axis=0, keepdims=True)
        z_ref[0:1, tok2] = 1.0 / z
        l_ref[0:1, tok2] = jnp.sum(chosen[0:16], axis=0, keepdims=True)
        for a in range(1, 8):
            l_ref[a:a + 1, tok2] = jnp.sum(chosen[8 + 8 * a:16 + 8 * a], axis=0, keepdims=True)
        l_ref[8:16, tok2] = chosen[72:80]
    for tok in chunks:
        rank1 = rk_ref[0, :, tok]
        rank2 = rk_ref[1, :, tok]
        lcnt = l_ref[:, tok]
        lc = jnp.zeros((PEER_NKEYS, LANES), F32)
        for a in range(PEER_TOPK):
            lc = jnp.where(rank1 == float(a), lcnt[a:a + 1], lc)
        lc_ref[:, tok] = lc
        f1_ref[:, tok] = jnp.exp(st_ref[0, :, tok] - t_ref[0, 0:1, tok]) * z_ref[0:1, tok]
        r2_ref[:, tok] = rank2
        f2_ref[:, tok] = jnp.where(rank2 < float(PEER_TOPK),
                                   jnp.exp(st_ref[1, :, tok] - t_ref[1, 0:1, tok]), 0.0)


def _topk(st):
    _, nk, t = st.shape
    tl = min(TOPK_TL, t)
    spec = pl.BlockSpec((None, nk, tl), lambda i, h: (h, 0, i))
    shp = jax.ShapeDtypeStruct((PEER_HEADS, nk, t), F32)
    return pl.pallas_call(
        _topk_body,
        grid=(t // tl, PEER_HEADS),
        in_specs=[pl.BlockSpec((2, nk, tl), lambda i, h: (h, 0, i))],
        out_specs=[spec, spec, spec, spec],
        out_shape=[shp, shp, shp, shp],
        scratch_shapes=[pltpu.VMEM((2, PEER_TOPK, tl), F32), pltpu.VMEM((2, nk, tl), F32),
                        pltpu.VMEM((PEER_TOPK, tl), F32), pltpu.VMEM((8, tl), F32),
                        pltpu.VMEM((80, tl), F32)],
        compiler_params=_cparams("arbitrary", "arbitrary"),
        name="topk",
    )(st)


def _peer_body(h2_ref, ed_ref, eut_ref, lc_ref, f1_ref, r2_ref, f2_ref, x1_ref, gf_ref,
               g2_ref, b2_ref, o_ref, acc_ref, *, ni):
    et = pl.program_id(1)

    @pl.when(et == 0)
    def _():
        acc_ref[...] = jnp.zeros(acc_ref.shape, F32)

    tm = h2_ref.shape[0]
    sub = BF16_SUBLANES
    kgroup = 2 * sub
    at = lax.dot_general(ed_ref[...], h2_ref[...], _NT, preferred_element_type=F32)
    cols = []
    for c in range(tm // LANES):
        tok = slice(c * LANES, (c + 1) * LANES)
        w_rows = {}
        for k0 in range(0, PEER_NKEYS, kgroup):
            subs = [slice(k0 + j * sub, k0 + (j + 1) * sub) for j in range(kgroup // sub)]
            rank = [[r2_ref[h, rk, tok].astype(BF16) for rk in subs] for h in range(PEER_HEADS)]
            val = [[f2_ref[h, rk, tok].astype(BF16) for rk in subs] for h in range(PEER_HEADS)]
            for ii in range(ni):
                g = [None] * len(subs)
                for h in range(PEER_HEADS):
                    cnt = jnp.broadcast_to(lc_ref[h, ii:ii + 1, tok], (sub, LANES)).astype(BF16)
                    fac = jnp.broadcast_to(f1_ref[h, ii:ii + 1, tok], (sub, LANES)).astype(BF16)
                    for j in range(len(subs)):
                        term = jnp.where(rank[h][j] < cnt, val[h][j], 0) * fac
                        g[j] = term if g[j] is None else g[j] + term
                for j, rk in enumerate(subs):
                    r0 = ii * PEER_NKEYS + rk.start
                    a = at[r0:r0 + sub, tok]
                    act = 0.5 * a * (1.0 + lax.erf(a * (2.0 ** -0.5)))
                    w_rows[r0] = g[j] * act.astype(BF16)
        cols.append(jnp.concatenate([w_rows[r0] for r0 in sorted(w_rows)], axis=0))
    wt = jnp.concatenate(cols, axis=1)
    acc_ref[...] += jnp.dot(eut_ref[...], wt, preferred_element_type=F32)

    @pl.when(et == pl.num_programs(1) - 1)
    def _():
        f = acc_ref[...].T
        y = DN_ALPHA * x1_ref[...] + (1.0 + gf_ref[...]) * f
        o_ref[...] = _ln(y) * g2_ref[...] + b2_ref[...]


def _peer(h2, ed, eut, lc, f1, r2, f2, x1, mod4, ln_g, ln_b, seq):
    t, d = h2.shape
    tm = min(PEER_TM, t)
    te = PEER_TE
    ni = te // PEER_NKEYS
    ne = PEER_N // te
    tpb = seq // tm
    lc4 = lc.reshape(PEER_HEADS, ne, ni, t)
    f14 = f1.reshape(PEER_HEADS, ne, ni, t)
    rowspec = pl.BlockSpec((PEER_HEADS, None, ni, tm), lambda i, e: (0, e, 0, i))
    colspec = pl.BlockSpec((PEER_HEADS, PEER_NKEYS, tm), lambda i, e: (0, 0, i))
    full = lambda *s: pl.BlockSpec(s, lambda i, e: (0,) * len(s))
    return pl.pallas_call(
        functools.partial(_peer_body, ni=ni),
        grid=(t // tm, ne),
        in_specs=[pl.BlockSpec((tm, d), lambda i, e: (i, 0)),
                  pl.BlockSpec((te, d), lambda i, e: (e, 0)),
                  pl.BlockSpec((d, te), lambda i, e: (0, e)),
                  rowspec, rowspec, colspec, colspec,
                  pl.BlockSpec((tm, d), lambda i, e: (i, 0)),
                  pl.BlockSpec((None, None, 1, d), lambda i, e: (i // tpb, 5, 0, 0)),
                  full(1, d), full(1, d)],
        out_specs=pl.BlockSpec((tm, d), lambda i, e: (i, 0)),
        out_shape=jax.ShapeDtypeStruct((t, d), F32),
        scratch_shapes=[pltpu.VMEM((d, tm), F32)],
        compiler_params=_cparams("arbitrary", "arbitrary"),
        name="peer",
    )(h2, ed, eut, lc4, f14, r2, f2, x1, mod4, ln_g, ln_b)


def kernel(x, c, w_ada, b_ada, w_in, w_out, na_rpb, t5_bias, lambda_q1, lambda_k1, lambda_q2,
           lambda_k2, diff_subln, ln1_g, ln1_b, w_query, sub_keys, expert_down, expert_up,
           ln2_g, ln2_b):
    b, seq, d = x.shape
    t = b * seq
    rows = seq // GRID_W
    assert d == D_MODEL and w_ada.shape[0] == DEPTH == 1
    assert rows % NA_QROWS == 0 and rows >= NA_KROWS and seq % DIFF_TQ == 0
    l = 0
    lambda_init = 0.8 - 0.6 * math.exp(-0.3 * l)
    xf = x.reshape(t, d)

    mod4 = _ada(c, w_ada[l], b_ada[l][None, :]).reshape(b, 6, 1, d)
    proj3 = _proj(xf, mod4, w_in[l].astype(BF16), seq).reshape(b, seq, IN_WIDTH)

    nb = _na_bias(na_rpb[l].reshape(-1), rows)
    o_na = _na(proj3, nb, rows).reshape(t, NA_WIDTH)

    dt = _t5_bias(t5_bias.reshape(-1), seq)
    o_df = _diff(proj3, dt, lambda_q1[l][None, :], lambda_k1[l][None, :], lambda_q2[l][None, :],
                 lambda_k2[l][None, :], diff_subln[l][None, :], lambda_init).reshape(t, DIFF_WIDTH)

    x1, h2, st = _mid(o_na, o_df, w_out[l].astype(BF16), xf, mod4, ln1_g[l][None, :],
                      ln1_b[l][None, :], w_query[l].astype(BF16), sub_keys[l].astype(BF16), seq)
    lc, f1, r2, f2 = _topk(st)
    out = _peer(h2, expert_down[l].astype(BF16), expert_up[l].astype(BF16).T, lc, f1, r2, f2,
                x1, mod4, ln2_g[l][None, :], ln2_b[l][None, :], seq)
    return out.reshape(b, seq, d)
```

```python
import functools
import math

import jax
import jax.numpy as jnp
from jax import lax
from jax.experimental import pallas as pl
from jax.experimental.pallas import tpu as pltpu

F32 = jnp.float32
BF16 = jnp.bfloat16

D_MODEL = 1024
DEPTH = 1
GRID_W = 64
NA_HEADS = 8
NA_HEAD_DIM = 64
NA_WIDTH = NA_HEADS * NA_HEAD_DIM
NA_WIN_ROWS = 8
NA_WIN_COLS = 16
DIFF_HEADS = 4
DIFF_HEAD_DIM = 64
DIFF_V_DIM = 2 * DIFF_HEAD_DIM
DIFF_WIDTH = DIFF_HEADS * DIFF_V_DIM
IN_WIDTH = 3 * NA_WIDTH + 3 * DIFF_WIDTH
T5_BUCKETS = 32
T5_MAX_DIST = 128
PEER_HEADS = 8
PEER_NKEYS = 128
PEER_N = PEER_NKEYS * PEER_NKEYS
PEER_TOPK = 16
LN_EPS = 1e-5
DN_ALPHA = (2.0 * DEPTH) ** 0.25
NEG_BIG = -1e30
LANES = 128
BF16_SUBLANES = 16

PROJ_TM = 512
NA_QROWS = 4
NA_KROWS = 12
DIFF_TQ = 256
DIFF_TILE = 256
MID_TM = 512
TOPK_TL = 2048
FAST_CHAINS = 4
PEER_TM = 512
PEER_TE = 2048
VMEM_LIMIT = 56 * 1024 * 1024

_NT = (((1,), (1,)), ((), ()))


def _cparams(*sem):
    return pltpu.CompilerParams(dimension_semantics=sem, vmem_limit_bytes=VMEM_LIMIT)


def _ln(x):
    mu = jnp.mean(x, axis=-1, keepdims=True)
    xc = x - mu
    var = jnp.mean(xc * xc, axis=-1, keepdims=True)
    return xc * lax.rsqrt(var + LN_EPS)


def _ada_body(c_ref, w_ref, b_ref, o_ref):
    c = c_ref[...]
    s = c * jax.nn.sigmoid(c)
    o_ref[...] = jnp.dot(s, w_ref[...], preferred_element_type=F32) + b_ref[...]


def _ada(c, w, bias):
    b, d = c.shape
    n = w.shape[1]
    return pl.pallas_call(
        _ada_body,
        grid=(n // d,),
        in_specs=[pl.BlockSpec((b, d), lambda j: (0, 0)),
                  pl.BlockSpec((d, d), lambda j: (0, j)),
                  pl.BlockSpec((1, d), lambda j: (0, j))],
        out_specs=pl.BlockSpec((b, d), lambda j: (0, j)),
        out_shape=jax.ShapeDtypeStruct((b, n), F32),
        compiler_params=_cparams("arbitrary"),
        name="ada",
    )(c, w, bias)


def _proj_body(x_ref, sc_ref, sh_ref, w_ref, o_ref):
    h = _ln(x_ref[...]) * (1.0 + sc_ref[...]) + sh_ref[...]
    o_ref[...] = jnp.dot(h.astype(BF16), w_ref[...], preferred_element_type=F32).astype(o_ref.dtype)


def _proj(xf, mod4, w, seq):
    t, d = xf.shape
    n = w.shape[1]
    tpb = seq // PROJ_TM
    return pl.pallas_call(
        _proj_body,
        grid=(t // PROJ_TM,),
        in_specs=[pl.BlockSpec((PROJ_TM, d), lambda i: (i, 0)),
                  pl.BlockSpec((None, None, 1, d), lambda i: (i // tpb, 1, 0, 0)),
                  pl.BlockSpec((None, None, 1, d), lambda i: (i // tpb, 0, 0, 0)),
                  pl.BlockSpec((d, n), lambda i: (0, 0))],
        out_specs=pl.BlockSpec((PROJ_TM, n), lambda i: (i, 0)),
        out_shape=jax.ShapeDtypeStruct((t, n), BF16),
        compiler_params=_cparams("arbitrary"),
        name="proj",
    )(xf, mod4, mod4, w)


def _na_row_offsets(rows):
    kr = min(NA_WIN_ROWS, rows)
    nrb = rows // NA_QROWS
    pats = []
    for rb in range(nrb):
        r0 = rb * NA_QROWS
        s = min(max(r0 - NA_WIN_ROWS // 2, 0), rows - NA_KROWS)
        pat = []
        for qr in range(NA_QROWS):
            r = r0 + qr
            rs = min(max(r - kr // 2, 0), rows - kr)
            pat.append(tuple((s + i) - r + (NA_WIN_ROWS - 1) if rs <= s + i < rs + kr else 15
                             for i in range(NA_KROWS)))
        pats.append(tuple(pat))
    variants = [pats[0], pats[1], pats[-1]]
    assert all(p == variants[1] for p in pats[1:-1])
    return variants


def _na_bias_body(rpb_ref, o_ref, tile_ref, *, variants):
    h = pl.program_id(0)
    nro = 2 * NA_WIN_ROWS - 1
    nco = 2 * NA_WIN_COLS - 1
    q = lax.broadcasted_iota(jnp.int32, (GRID_W, LANES), 0)
    kk = lax.broadcasted_iota(jnp.int32, (GRID_W, LANES), 1)
    k = jnp.bitwise_and(kk, GRID_W - 1)
    col_start = jnp.clip(q - NA_WIN_COLS // 2, 0, GRID_W - NA_WIN_COLS)
    valid = jnp.logical_and(k >= col_start, k < col_start + NA_WIN_COLS)
    co = jnp.clip(k - q, -(NA_WIN_COLS - 1), NA_WIN_COLS - 1) + (NA_WIN_COLS - 1)
    neg = jnp.full((GRID_W, LANES), NEG_BIG, F32)
    for ro in range(nro):
        acc = neg
        for cc in range(nco):
            acc = jnp.where(co == cc, rpb_ref[(h * nro + ro) * nco + cc], acc)
        tile_ref[ro] = jnp.where(valid, acc, NEG_BIG)
    tile_ref[nro] = neg
    left = kk < GRID_W
    for v, pat in enumerate(variants):
        for qr in range(NA_QROWS):
            for kp in range(NA_KROWS // 2):
                a, b = pat[qr][2 * kp], pat[qr][2 * kp + 1]
                o_ref[v, qr * GRID_W:(qr + 1) * GRID_W, kp * LANES:(kp + 1) * LANES] = (
                    jnp.where(left, tile_ref[a], tile_ref[b]))


def _na_bias(rpb_flat, rows):
    variants = _na_row_offsets(rows)
    nq, nk = NA_QROWS * GRID_W, NA_KROWS * GRID_W
    return pl.pallas_call(
        functools.partial(_na_bias_body, variants=variants),
        grid=(NA_HEADS,),
        in_specs=[pl.BlockSpec(memory_space=pltpu.SMEM)],
        out_specs=pl.BlockSpec((3, None, nq, nk), lambda h: (0, h, 0, 0)),
        out_shape=jax.ShapeDtypeStruct((3, NA_HEADS, nq, nk), F32),
        scratch_shapes=[pltpu.VMEM((2 * NA_WIN_ROWS, GRID_W, LANES), F32)],
        compiler_params=_cparams("arbitrary"),
        name="na_bias",
    )(rpb_flat)


def _half_masks(value=1.0):
    lane = lax.broadcasted_iota(jnp.int32, (1, LANES), 1)
    lo = jnp.where(lane < LANES // 2, value, 0.0).astype(BF16)
    hi = jnp.where(lane < LANES // 2, 0.0, value).astype(BF16)
    return lo, hi


def _na_body(q_ref, k_ref, v_ref, b_ref, o_ref, *, rows):
    rb = pl.program_id(0)
    s_row = jnp.clip(rb * NA_QROWS - NA_WIN_ROWS // 2, 0, rows - NA_KROWS)
    start = pl.multiple_of(s_row * GRID_W, GRID_W)
    nk = NA_KROWS * GRID_W
    masks = list(zip(_half_masks(NA_HEAD_DIM ** -0.5), _half_masks()))
    for hp in range(NA_HEADS // 2):
        cols = slice(hp * LANES, (hp + 1) * LANES)
        q = q_ref[:, cols]
        kw = k_ref[pl.ds(start, nk), cols]
        vw = v_ref[pl.ds(start, nk), cols]
        acc = jnp.zeros(q.shape, F32)
        for h, (mq, mv) in enumerate(masks):
            s = lax.dot_general(q * mq, kw, _NT, preferred_element_type=F32) + b_ref[2 * hp + h]
            p = jnp.exp(s - jnp.max(s, axis=-1, keepdims=True))
            l = jnp.sum(p, axis=-1, keepdims=True)
            o = jnp.dot(p.astype(BF16), vw * mv, preferred_element_type=F32)
            acc = acc + o * (1.0 / l)
        o_ref[:, cols] = acc.astype(o_ref.dtype)


def _na(proj3, nb, rows):
    b, seq, _ = proj3.shape
    nrb = rows // NA_QROWS
    nq, nk = NA_QROWS * GRID_W, NA_KROWS * GRID_W

    def variant(rb):
        return jnp.where(rb == 0, 0, jnp.where(rb == nrb - 1, 2, 1))

    return pl.pallas_call(
        functools.partial(_na_body, rows=rows),
        grid=(nrb, b),
        in_specs=[pl.BlockSpec((None, nq, NA_WIDTH), lambda rb, bi: (bi, rb, 0)),
                  pl.BlockSpec((None, seq, NA_WIDTH), lambda rb, bi: (bi, 0, 1)),
                  pl.BlockSpec((None, seq, NA_WIDTH), lambda rb, bi: (bi, 0, 2)),
                  pl.BlockSpec((None, NA_HEADS, nq, nk), lambda rb, bi: (variant(rb), 0, 0, 0))],
        out_specs=pl.BlockSpec((None, nq, NA_WIDTH), lambda rb, bi: (bi, rb, 0)),
        out_shape=jax.ShapeDtypeStruct((b, seq, NA_WIDTH), BF16),
        compiler_params=_cparams("arbitrary", "arbitrary"),
        name="na",
    )(proj3, proj3, proj3, nb)


def _t5_thresholds():
    nb = T5_BUCKETS // 2
    me = nb // 2
    span = nb - me
    ratio = T5_MAX_DIST // me
    out = []
    for m in range(1, span):
        n = me
        while n ** span < me ** span * ratio ** m:
            n += 1
        out.append(n)
    return out


def _t5_bias_body(t5_ref, o_ref, *, nq):
    h = pl.program_id(0)
    nb = T5_BUCKETS // 2
    me = nb // 2
    thresholds = _t5_thresholds()
    assert DIFF_TILE >= thresholds[-1]
    block = pl.program_id(1) - (nq - 1)

    @pl.when(jnp.abs(block) <= 1)
    def _():
        q = lax.broadcasted_iota(jnp.int32, (DIFF_TILE, DIFF_TILE), 0)
        k = lax.broadcasted_iota(jnp.int32, (DIFF_TILE, DIFF_TILE), 1)
        rel = block * DIFF_TILE + k - q
        n = jnp.abs(rel)
        large = jnp.full(n.shape, me, jnp.int32)
        for thr in thresholds:
            large = large + jnp.where(n >= thr, 1, 0)
        bkt = jnp.where(rel > 0, nb, 0) + jnp.where(n < me, n, large)
        acc = jnp.zeros(bkt.shape, F32)
        for bb in range(T5_BUCKETS):
            acc = jnp.where(bkt == bb, t5_ref[bb * DIFF_HEADS + h], acc)
        o_ref[...] = acc

    @pl.when(jnp.abs(block) > 1)
    def _():
        far = jnp.where(block > 0, t5_ref[(T5_BUCKETS - 1) * DIFF_HEADS + h],
                        t5_ref[(nb - 1) * DIFF_HEADS + h])
        o_ref[...] = jnp.full(o_ref.shape, far, F32)


def _t5_bias(t5_flat, seq):
    nq = seq // DIFF_TILE
    nt = 2 * nq - 1
    return pl.pallas_call(
        functools.partial(_t5_bias_body, nq=nq),
        grid=(DIFF_HEADS, nt),
        in_specs=[pl.BlockSpec(memory_space=pltpu.SMEM)],
        out_specs=pl.BlockSpec((None, None, DIFF_TILE, DIFF_TILE), lambda h, t: (h, t, 0, 0)),
        out_shape=jax.ShapeDtypeStruct((DIFF_HEADS, nt, DIFF_TILE, DIFF_TILE), F32),
        compiler_params=_cparams("arbitrary", "arbitrary"),
        name="t5_bias",
    )(t5_flat)


def _diff_body(q_ref, k_ref, v_ref, dt_ref, lq1_ref, lk1_ref, lq2_ref, lk2_ref, g_ref, o_ref,
               *, nkb, lambda_init):
    qi = pl.program_id(2)
    q = q_ref[...]
    nsub = DIFF_TQ // DIFF_TILE
    lam = (jnp.exp(jnp.sum(lq1_ref[...] * lk1_ref[...], axis=-1, keepdims=True))
           - jnp.exp(jnp.sum(lq2_ref[...] * lk2_ref[...], axis=-1, keepdims=True)) + lambda_init)
    maps = []
    for m in _half_masks(DIFF_HEAD_DIM ** -0.5):
        qm = q * m
        pieces = []
        for kb in range(nkb):
            s = lax.dot_general(qm, k_ref[kb * DIFF_TILE:(kb + 1) * DIFF_TILE, :], _NT,
                                preferred_element_type=F32)
            bias = [dt_ref[nkb - 1 - (nsub * qi + hq) + kb] for hq in range(nsub)]
            pieces.append(s + jnp.concatenate(bias, axis=0))
        s = jnp.concatenate(pieces, axis=1)
        p = jnp.exp(s - jnp.max(s, axis=-1, keepdims=True))
        maps.append((p, jnp.sum(p, axis=-1, keepdims=True)))
    (p1, l1), (p2, l2) = maps
    w = p1 * (1.0 / l1) - p2 * (lam / l2)
    o = jnp.dot(w.astype(BF16), v_ref[...], preferred_element_type=F32)
    o = o * lax.rsqrt(jnp.mean(o * o, axis=-1, keepdims=True) + LN_EPS) * g_ref[...]
    o_ref[...] = (o * (1.0 - lambda_init)).astype(o_ref.dtype)


def _diff(proj3, dt, lq1, lk1, lq2, lk2, subln, lambda_init):
    b, seq, _ = proj3.shape
    nkb = seq // DIFF_TILE
    nt = 2 * nkb - 1
    c0 = 3 * NA_WIDTH // LANES
    vec = lambda n: pl.BlockSpec((1, n), lambda h, bi, qi: (0, 0))
    return pl.pallas_call(
        functools.partial(_diff_body, nkb=nkb, lambda_init=lambda_init),
        grid=(DIFF_HEADS, b, seq // DIFF_TQ),
        in_specs=[pl.BlockSpec((None, DIFF_TQ, LANES), lambda h, bi, qi: (bi, qi, c0 + h)),
                  pl.BlockSpec((None, seq, LANES), lambda h, bi, qi: (bi, 0, c0 + DIFF_HEADS + h)),
                  pl.BlockSpec((None, seq, LANES), lambda h, bi, qi: (bi, 0, c0 + 2 * DIFF_HEADS + h)),
                  pl.BlockSpec((None, nt, DIFF_TILE, DIFF_TILE), lambda h, bi, qi: (h, 0, 0, 0)),
                  vec(DIFF_HEAD_DIM), vec(DIFF_HEAD_DIM), vec(DIFF_HEAD_DIM), vec(DIFF_HEAD_DIM),
                  vec(DIFF_V_DIM)],
        out_specs=pl.BlockSpec((None, DIFF_TQ, LANES), lambda h, bi, qi: (bi, qi, h)),
        out_shape=jax.ShapeDtypeStruct((b, seq, DIFF_WIDTH), BF16),
        compiler_params=_cparams("arbitrary", "arbitrary", "arbitrary"),
        name="diff",
    )(proj3, proj3, proj3, dt, lq1, lk1, lq2, lk2, subln)


def _mid_body(ona_ref, odf_ref, wout_ref, x_ref, ga_ref, scf_ref, shf_ref, g1_ref, b1_ref,
              wq_ref, sk_ref, x1_ref, h2_ref, st_ref):
    mix = (jnp.dot(ona_ref[...], wout_ref[0:NA_WIDTH, :], preferred_element_type=F32)
           + jnp.dot(odf_ref[...], wout_ref[NA_WIDTH:, :], preferred_element_type=F32))
    y = DN_ALPHA * x_ref[...] + (1.0 + ga_ref[...]) * mix
    x1 = _ln(y) * g1_ref[...] + b1_ref[...]
    x1_ref[...] = x1
    h2 = (_ln(x1) * (1.0 + scf_ref[...]) + shf_ref[...]).astype(BF16)
    h2_ref[...] = h2
    qp = jnp.dot(h2, wq_ref[...], preferred_element_type=F32).astype(BF16)
    for hp in range(2 * PEER_HEADS):
        st_ref[hp] = lax.dot_general(sk_ref[hp % 2], qp[:, hp * PEER_NKEYS:(hp + 1) * PEER_NKEYS],
                                     _NT, preferred_element_type=F32)


def _mid(o_na, o_df, w_out, xf, mod4, ln_g, ln_b, w_query, sub_keys, seq):
    t, d = xf.shape
    tpb = seq // MID_TM
    nqk = w_query.shape[1]
    modspec = lambda j: pl.BlockSpec((None, None, 1, d), lambda i: (i // tpb, j, 0, 0))
    full = lambda *s: pl.BlockSpec(s, lambda i: (0,) * len(s))
    return pl.pallas_call(
        _mid_body,
        grid=(t // MID_TM,),
        in_specs=[pl.BlockSpec((MID_TM, NA_WIDTH), lambda i: (i, 0)),
                  pl.BlockSpec((MID_TM, DIFF_WIDTH), lambda i: (i, 0)),
                  full(d, d),
                  pl.BlockSpec((MID_TM, d), lambda i: (i, 0)),
                  modspec(2), modspec(4), modspec(3),
                  full(1, d), full(1, d),
                  full(d, nqk),
                  full(2, PEER_NKEYS, PEER_NKEYS)],
        out_specs=[pl.BlockSpec((MID_TM, d), lambda i: (i, 0)),
                   pl.BlockSpec((MID_TM, d), lambda i: (i, 0)),
                   pl.BlockSpec((2 * PEER_HEADS, PEER_NKEYS, MID_TM), lambda i: (0, 0, i))],
        out_shape=[jax.ShapeDtypeStruct((t, d), F32),
                   jax.ShapeDtypeStruct((t, d), BF16),
                   jax.ShapeDtypeStruct((2 * PEER_HEADS, PEER_NKEYS, t), F32)],
        compiler_params=_cparams("arbitrary"),
        name="mid",
    )(o_na, o_df, w_out, xf, mod4, mod4, mod4, ln_g, ln_b, w_query, sub_keys)


def _extract_top(work, order, tie_rule, out_ref=None, tok=None):
    rank = jnp.full(work.shape, float(PEER_TOPK), F32)
    big = float(PEER_N)
    for r in range(PEER_TOPK):
        m = jnp.max(work, axis=0, keepdims=True)
        sel = work == m
        if tie_rule:
            first = jnp.min(jnp.where(sel, order, big), axis=0, keepdims=True)
            sel = order == first
        rank = jnp.where(sel, float(r), rank)
        work = jnp.where(sel, -jnp.inf, work)
        if out_ref is not None:
            out_ref[r:r + 1, tok] = m
    return rank


def _column_max(x):
    slabs = [x[i:i + 8] for i in range(0, x.shape[0], 8)]
    while len(slabs) > 1:
        slabs = [jnp.maximum(slabs[i], slabs[i + 1]) if i + 1 < len(slabs) else slabs[i]
                 for i in range(0, len(slabs), 2)]
    return jnp.max(slabs[0], axis=0, keepdims=True)


def _pair_cells(op, u1, u2):
    g = [op(u1[0:1], u2[0:8]), op(u1[0:1], u2[8:16])]
    for a in range(1, 8):
        g.append(op(u1[a:a + 1], u2[0:8]))
    g.append(op(u1[8:16], u2[0:1]))
    return jnp.concatenate(g, axis=0)


def _topk_body(st_ref, lc_ref, f1_ref, r2_ref, f2_ref, t_ref, rk_ref, l_ref, z_ref, w2_ref):
    tl = st_ref.shape[-1]
    chunks = [slice(c * LANES, (c + 1) * LANES) for c in range(tl // LANES)]
    key_id = lax.broadcasted_iota(jnp.int32, (PEER_NKEYS, LANES), 0).astype(F32)

    def half_ranks(tie_rule):
        for tok in chunks:
            for p in range(2):
                rk_ref[p, :, tok] = _extract_top(st_ref[p, :, tok], key_id, tie_rule,
                                                 t_ref.at[p], tok)

    jobs = [(p, tok) for tok in chunks for p in range(2)]
    for j0 in range(0, len(jobs), FAST_CHAINS):
        group = jobs[j0:j0 + FAST_CHAINS]
        work = [st_ref[p, :, tok] for p, tok in group]
        for p, tok in group:
            rk_ref[p, :, tok] = jnp.full((PEER_NKEYS, LANES), float(PEER_TOPK), F32)
        for r in range(PEER_TOPK):
            for i, (p, tok) in enumerate(group):
                m = _column_max(work[i])
                sel = work[i] == m
                rk_ref[p, :, tok] = jnp.where(sel, float(r), rk_ref[p, :, tok])
                work[i] = jnp.where(sel, -jnp.inf, work[i])
                t_ref[p, r:r + 1, tok] = m
    taken = jnp.where(rk_ref[...] < float(PEER_TOPK), 1.0, 0.0)
    most = jnp.max(jnp.sum(taken, axis=1, keepdims=True))

    @pl.when(most > float(PEER_TOPK))
    def _():
        half_ranks(True)

    wide = 4 * LANES
    row = lax.broadcasted_iota(jnp.int32, (80, wide), 0)
    mid = row - 16
    cell_id = jnp.where(row < 16, row,
                        jnp.where(row < 72,
                                  (jnp.right_shift(mid, 3) + 1) * 16 + jnp.bitwise_and(mid, 7),
                                  (row - 64) * 16)).astype(F32)
    for c0 in range(0, tl, wide):
        tok2 = slice(c0, c0 + wide)
        t1 = t_ref[0, :, tok2]
        t2 = t_ref[1, :, tok2]
        work = _pair_cells(jnp.add, t1, t2)
        for _ in range(PEER_TOPK):
            work = jnp.where(work == _column_max(work), -jnp.inf, work)
        w2_ref[:, tok2] = work
        taken = jnp.sum(jnp.where(work == -jnp.inf, 1.0, 0.0), axis=0, keepdims=True)

        @pl.when(jnp.max(taken) > float(PEER_TOPK))
        def _():
            again = _pair_cells(jnp.add, t_ref[0, :, tok2], t_ref[1, :, tok2])
            for _ in range(PEER_TOPK):
                m = jnp.max(again, axis=0, keepdims=True)
                first = jnp.min(jnp.where(again == m, cell_id, float(PEER_N)), axis=0,
                                keepdims=True)
                again = jnp.where(cell_id == first, -jnp.inf, again)
            w2_ref[:, tok2] = again

        chosen = jnp.where(w2_ref[:, tok2] == -jnp.inf, 1.0, 0.0)
        e1 = jnp.exp(t1 - t1[0:1])
        e2 = jnp.exp(t2 - t2[0:1])
        z = jnp.sum(chosen * _pair_cells(jnp.multiply, e1, e2), axis=0, keepdims=True)
        z_ref[0:1, tok2] = 1.0 / z
        l_ref[0:1, tok2] = jnp.sum(chosen[0:16], axis=0, keepdims=True)
        for a in range(1, 8):
            l_ref[a:a + 1, tok2] = jnp.sum(chosen[8 + 8 * a:16 + 8 * a], axis=0, keepdims=True)
        l_ref[8:16, tok2] = chosen[72:80]
    for tok in chunks:
        rank1 = rk_ref[0, :, tok]
        rank2 = rk_ref[1, :, tok]
        lcnt = l_ref[:, tok]
        lc = jnp.zeros((PEER_NKEYS, LANES), F32)
        for a in range(PEER_TOPK):
            lc = jnp.where(rank1 == float(a), lcnt[a:a + 1], lc)
        lc_ref[:, tok] = lc
        f1_ref[:, tok] = jnp.exp(st_ref[0, :, tok] - t_ref[0, 0:1, tok]) * z_ref[0:1, tok]
        r2_ref[:, tok] = rank2
        f2_ref[:, tok] = jnp.where(rank2 < float(PEER_TOPK),
                                   jnp.exp(st_ref[1, :, tok] - t_ref[1, 0:1, tok]), 0.0)


def _topk(st):
    _, nk, t = st.shape
    tl = min(TOPK_TL, t)
    spec = pl.BlockSpec((None, nk, tl), lambda i, h: (h, 0, i))
    shp = jax.ShapeDtypeStruct((PEER_HEADS, nk, t), F32)
    return pl.pallas_call(
        _topk_body,
        grid=(t // tl, PEER_HEADS),
        in_specs=[pl.BlockSpec((2, nk, tl), lambda i, h: (h, 0, i))],
        out_specs=[spec, spec, spec, spec],
        out_shape=[shp, shp, shp, shp],
        scratch_shapes=[pltpu.VMEM((2, PEER_TOPK, tl), F32), pltpu.VMEM((2, nk, tl), F32),
                        pltpu.VMEM((PEER_TOPK, tl), F32), pltpu.VMEM((8, tl), F32),
                        pltpu.VMEM((80, tl), F32)],
        compiler_params=_cparams("arbitrary", "arbitrary"),
        name="topk",
    )(st)


def _peer_body(h2_ref, ed_ref, eut_ref, lc_ref, f1_ref, r2_ref, f2_ref, x1_ref, gf_ref,
               g2_ref, b2_ref, o_ref, acc_ref, *, ni):
    et = pl.program_id(1)

    @pl.when(et == 0)
    def _():
        acc_ref[...] = jnp.zeros(acc_ref.shape, F32)

    tm = h2_ref.shape[0]
    sub = BF16_SUBLANES
    kgroup = 2 * sub
    at = lax.dot_general(ed_ref[...], h2_ref[...], _NT, preferred_element_type=F32)
    cols = []
    for c in range(tm // LANES):
        tok = slice(c * LANES, (c + 1) * LANES)
        w_rows = {}
        for k0 in range(0, PEER_NKEYS, kgroup):
            subs = [slice(k0 + j * sub, k0 + (j + 1) * sub) for j in range(kgroup // sub)]
            rank = [[r2_ref[h, rk, tok].astype(BF16) for rk in subs] for h in range(PEER_HEADS)]
            val = [[f2_ref[h, rk, tok].astype(BF16) for rk in subs] for h in range(PEER_HEADS)]
            for ii in range(ni):
                g = [None] * len(subs)
                for h in range(PEER_HEADS):
                    cnt = jnp.broadcast_to(lc_ref[h, ii:ii + 1, tok], (sub, LANES)).astype(BF16)
                    fac = jnp.broadcast_to(f1_ref[h, ii:ii + 1, tok], (sub, LANES)).astype(BF16)
                    for j in range(len(subs)):
                        term = jnp.where(rank[h][j] < cnt, val[h][j], 0) * fac
                        g[j] = term if g[j] is None else g[j] + term
                for j, rk in enumerate(subs):
                    r0 = ii * PEER_NKEYS + rk.start
                    a = at[r0:r0 + sub, tok]
                    act = 0.5 * a * (1.0 + lax.erf(a * (2.0 ** -0.5)))
                    w_rows[r0] = g[j] * act.astype(BF16)
        cols.append(jnp.concatenate([w_rows[r0] for r0 in sorted(w_rows)], axis=0))
    wt = jnp.concatenate(cols, axis=1)
    acc_ref[...] += jnp.dot(eut_ref[...], wt, preferred_element_type=F32)

    @pl.when(et == pl.num_programs(1) - 1)
    def _():
        f = acc_ref[...].T
        y = DN_ALPHA * x1_ref[...] + (1.0 + gf_ref[...]) * f
        o_ref[...] = _ln(y) * g2_ref[...] + b2_ref[...]


def _peer(h2, ed, eut, lc, f1, r2, f2, x1, mod4, ln_g, ln_b, seq):
    t, d = h2.shape
    tm = min(PEER_TM, t)
    te = PEER_TE
    ni = te // PEER_NKEYS
    ne = PEER_N // te
    tpb = seq // tm
    lc4 = lc.reshape(PEER_HEADS, ne, ni, t)
    f14 = f1.reshape(PEER_HEADS, ne, ni, t)
    rowspec = pl.BlockSpec((PEER_HEADS, None, ni, tm), lambda i, e: (0, e, 0, i))
    colspec = pl.BlockSpec((PEER_HEADS, PEER_NKEYS, tm), lambda i, e: (0, 0, i))
    full = lambda *s: pl.BlockSpec(s, lambda i, e: (0,) * len(s))
    return pl.pallas_call(
        functools.partial(_peer_body, ni=ni),
        grid=(t // tm, ne),
        in_specs=[pl.BlockSpec((tm, d), lambda i, e: (i, 0)),
                  pl.BlockSpec((te, d), lambda i, e: (e, 0)),
                  pl.BlockSpec((d, te), lambda i, e: (0, e)),
                  rowspec, rowspec, colspec, colspec,
                  pl.BlockSpec((tm, d), lambda i, e: (i, 0)),
                  pl.BlockSpec((None, None, 1, d), lambda i, e: (i // tpb, 5, 0, 0)),
                  full(1, d), full(1, d)],
        out_specs=pl.BlockSpec((tm, d), lambda i, e: (i, 0)),
        out_shape=jax.ShapeDtypeStruct((t, d), F32),
        scratch_shapes=[pltpu.VMEM((d, tm), F32)],
        compiler_params=_cparams("arbitrary", "arbitrary"),
        name="peer",
    )(h2, ed, eut, lc4, f14, r2, f2, x1, mod4, ln_g, ln_b)


def kernel(x, c, w_ada, b_ada, w_in, w_out, na_rpb, t5_bias, lambda_q1, lambda_k1, lambda_q2,
           lambda_k2, diff_subln, ln1_g, ln1_b, w_query, sub_keys, expert_down, expert_up,
           ln2_g, ln2_b):
    b, seq, d = x.shape
    t = b * seq
    rows = seq // GRID_W
    assert d == D_MODEL and w_ada.shape[0] == DEPTH == 1
    assert rows % NA_QROWS == 0 and rows >= NA_KROWS and seq % DIFF_TQ == 0
    l = 0
    lambda_init = 0.8 - 0.6 * math.exp(-0.3 * l)
    xf = x.reshape(t, d)

    mod4 = _ada(c, w_ada[l], b_ada[l][None, :]).reshape(b, 6, 1, d)
    proj3 = _proj(xf, mod4, w_in[l].astype(BF16), seq).reshape(b, seq, IN_WIDTH)

    nb = _na_bias(na_rpb[l].reshape(-1), rows)
    o_na = _na(proj3, nb, rows).reshape(t, NA_WIDTH)

    dt = _t5_bias(t5_bias.reshape(-1), seq)
    o_df = _diff(proj3, dt, lambda_q1[l][None, :], lambda_k1[l][None, :], lambda_q2[l][None, :],
                 lambda_k2[l][None, :], diff_subln[l][None, :], lambda_init).reshape(t, DIFF_WIDTH)

    x1, h2, st = _mid(o_na, o_df, w_out[l].astype(BF16), xf, mod4, ln1_g[l][None, :],
                      ln1_b[l][None, :], w_query[l].astype(BF16), sub_keys[l].astype(BF16), seq)
    lc, f1, r2, f2 = _topk(st)
    out = _peer(h2, expert_down[l].astype(BF16), expert_up[l].astype(BF16).T, lc, f1, r2, f2,
                x1, mod4, ln2_g[l][None, :], ln2_b[l][None, :], seq)
    return out.reshape(b, seq, d)
```

```python
import functools
import math

import jax
import jax.numpy as jnp
from jax import lax
from jax.experimental import pallas as pl
from jax.experimental.pallas import tpu as pltpu

F32 = jnp.float32
BF16 = jnp.bfloat16

D_MODEL = 1024
DEPTH = 1
GRID_W = 64
NA_HEADS = 8
NA_HEAD_DIM = 64
NA_WIDTH = NA_HEADS * NA_HEAD_DIM
NA_WIN_ROWS = 8
NA_WIN_COLS = 16
DIFF_HEADS = 4
DIFF_HEAD_DIM = 64
DIFF_V_DIM = 2 * DIFF_HEAD_DIM
DIFF_WIDTH = DIFF_HEADS * DIFF_V_DIM
IN_WIDTH = 3 * NA_WIDTH + 3 * DIFF_WIDTH
T5_BUCKETS = 32
T5_MAX_DIST = 128
PEER_HEADS = 8
PEER_NKEYS = 128
PEER_N = PEER_NKEYS * PEER_NKEYS
PEER_TOPK = 16
LN_EPS = 1e-5
DN_ALPHA = (2.0 * DEPTH) ** 0.25
NEG_BIG = -1e30
LANES = 128
BF16_SUBLANES = 16

PROJ_TM = 512
NA_QROWS = 4
NA_KROWS = 12
DIFF_TQ = 256
DIFF_TILE = 256
MID_TM = 512
TOPK_TL = 1024
FAST_CHAINS = 4
PEER_TM = 512
PEER_TE = 2048
VMEM_LIMIT = 56 * 1024 * 1024

_NT = (((1,), (1,)), ((), ()))


def _cparams(*sem):
    return pltpu.CompilerParams(dimension_semantics=sem, vmem_limit_bytes=VMEM_LIMIT)


def _ln(x):
    mu = jnp.mean(x, axis=-1, keepdims=True)
    xc = x - mu
    var = jnp.mean(xc * xc, axis=-1, keepdims=True)
    return xc * lax.rsqrt(var + LN_EPS)


def _ada_body(c_ref, w_ref, b_ref, o_ref):
    c = c_ref[...]
    s = c * jax.nn.sigmoid(c)
    o_ref[...] = jnp.dot(s, w_ref[...], preferred_element_type=F32) + b_ref[...]


def _ada(c, w, bias):
    b, d = c.shape
    n = w.shape[1]
    return pl.pallas_call(
        _ada_body,
        grid=(n // d,),
        in_specs=[pl.BlockSpec((b, d), lambda j: (0, 0)),
                  pl.BlockSpec((d, d), lambda j: (0, j)),
                  pl.BlockSpec((1, d), lambda j: (0, j))],
        out_specs=pl.BlockSpec((b, d), lambda j: (0, j)),
        out_shape=jax.ShapeDtypeStruct((b, n), F32),
        compiler_params=_cparams("arbitrary"),
        name="ada",
    )(c, w, bias)


def _proj_body(x_ref, sc_ref, sh_ref, w_ref, o_ref):
    h = _ln(x_ref[...]) * (1.0 + sc_ref[...]) + sh_ref[...]
    o_ref[...] = jnp.dot(h.astype(BF16), w_ref[...], preferred_element_type=F32).astype(o_ref.dtype)


def _proj(xf, mod4, w, seq):
    t, d = xf.shape
    n = w.shape[1]
    tpb = seq // PROJ_TM
    return pl.pallas_call(
        _proj_body,
        grid=(t // PROJ_TM,),
        in_specs=[pl.BlockSpec((PROJ_TM, d), lambda i: (i, 0)),
                  pl.BlockSpec((None, None, 1, d), lambda i: (i // tpb, 1, 0, 0)),
                  pl.BlockSpec((None, None, 1, d), lambda i: (i // tpb, 0, 0, 0)),
                  pl.BlockSpec((d, n), lambda i: (0, 0))],
        out_specs=pl.BlockSpec((PROJ_TM, n), lambda i: (i, 0)),
        out_shape=jax.ShapeDtypeStruct((t, n), BF16),
        compiler_params=_cparams("arbitrary"),
        name="proj",
    )(xf, mod4, mod4, w)


def _na_row_offsets(rows):
    kr = min(NA_WIN_ROWS, rows)
    nrb = rows // NA_QROWS
    pats = []
    for rb in range(nrb):
        r0 = rb * NA_QROWS
        s = min(max(r0 - NA_WIN_ROWS // 2, 0), rows - NA_KROWS)
        pat = []
        for qr in range(NA_QROWS):
            r = r0 + qr
            rs = min(max(r - kr // 2, 0), rows - kr)
            pat.append(tuple((s + i) - r + (NA_WIN_ROWS - 1) if rs <= s + i < rs + kr else 15
                             for i in range(NA_KROWS)))
        pats.append(tuple(pat))
    variants = [pats[0], pats[1], pats[-1]]
    assert all(p == variants[1] for p in pats[1:-1])
    return variants


def _na_bias_body(rpb_ref, o_ref, tile_ref, *, variants):
    h = pl.program_id(0)
    nro = 2 * NA_WIN_ROWS - 1
    nco = 2 * NA_WIN_COLS - 1
    q = lax.broadcasted_iota(jnp.int32, (GRID_W, LANES), 0)
    kk = lax.broadcasted_iota(jnp.int32, (GRID_W, LANES), 1)
    k = jnp.bitwise_and(kk, GRID_W - 1)
    col_start = jnp.clip(q - NA_WIN_COLS // 2, 0, GRID_W - NA_WIN_COLS)
    valid = jnp.logical_and(k >= col_start, k < col_start + NA_WIN_COLS)
    co = jnp.clip(k - q, -(NA_WIN_COLS - 1), NA_WIN_COLS - 1) + (NA_WIN_COLS - 1)
    neg = jnp.full((GRID_W, LANES), NEG_BIG, F32)
    for ro in range(nro):
        acc = neg
        for cc in range(nco):
            acc = jnp.where(co == cc, rpb_ref[(h * nro + ro) * nco + cc], acc)
        tile_ref[ro] = jnp.where(valid, acc, NEG_BIG)
    tile_ref[nro] = neg
    left = kk < GRID_W
    for v, pat in enumerate(variants):
        for qr in range(NA_QROWS):
            for kp in range(NA_KROWS // 2):
                a, b = pat[qr][2 * kp], pat[qr][2 * kp + 1]
                o_ref[v, qr * GRID_W:(qr + 1) * GRID_W, kp * LANES:(kp + 1) * LANES] = (
                    jnp.where(left, tile_ref[a], tile_ref[b]))


def _na_bias(rpb_flat, rows):
    variants = _na_row_offsets(rows)
    nq, nk = NA_QROWS * GRID_W, NA_KROWS * GRID_W
    return pl.pallas_call(
        functools.partial(_na_bias_body, variants=variants),
        grid=(NA_HEADS,),
        in_specs=[pl.BlockSpec(memory_space=pltpu.SMEM)],
        out_specs=pl.BlockSpec((3, None, nq, nk), lambda h: (0, h, 0, 0)),
        out_shape=jax.ShapeDtypeStruct((3, NA_HEADS, nq, nk), F32),
        scratch_shapes=[pltpu.VMEM((2 * NA_WIN_ROWS, GRID_W, LANES), F32)],
        compiler_params=_cparams("arbitrary"),
        name="na_bias",
    )(rpb_flat)


def _half_masks(value=1.0):
    lane = lax.broadcasted_iota(jnp.int32, (1, LANES), 1)
    lo = jnp.where(lane < LANES // 2, value, 0.0).astype(BF16)
    hi = jnp.where(lane < LANES // 2, 0.0, value).astype(BF16)
    return lo, hi


def _na_body(q_ref, k_ref, v_ref, b_ref, o_ref, *, rows):
    rb = pl.program_id(0)
    s_row = jnp.clip(rb * NA_QROWS - NA_WIN_ROWS // 2, 0, rows - NA_KROWS)
    start = pl.multiple_of(s_row * GRID_W, GRID_W)
    nk = NA_KROWS * GRID_W
    masks = list(zip(_half_masks(NA_HEAD_DIM ** -0.5), _half_masks()))
    for hp in range(NA_HEADS // 2):
        cols = slice(hp * LANES, (hp + 1) * LANES)
        q = q_ref[:, cols]
        kw = k_ref[pl.ds(start, nk), cols]
        vw = v_ref[pl.ds(start, nk), cols]
        acc = jnp.zeros(q.shape, F32)
        for h, (mq, mv) in enumerate(masks):
            s = lax.dot_general(q * mq, kw, _NT, preferred_element_type=F32) + b_ref[2 * hp + h]
            p = jnp.exp(s - jnp.max(s, axis=-1, keepdims=True))
            l = jnp.sum(p, axis=-1, keepdims=True)
            o = jnp.dot(p.astype(BF16), vw * mv, preferred_element_type=F32)
            acc = acc + o * (1.0 / l)
        o_ref[:, cols] = acc.astype(o_ref.dtype)


def _na(proj3, nb, rows):
    b, seq, _ = proj3.shape
    nrb = rows // NA_QROWS
    nq, nk = NA_QROWS * GRID_W, NA_KROWS * GRID_W

    def variant(rb):
        return jnp.where(rb == 0, 0, jnp.where(rb == nrb - 1, 2, 1))

    return pl.pallas_call(
        functools.partial(_na_body, rows=rows),
        grid=(nrb, b),
        in_specs=[pl.BlockSpec((None, nq, NA_WIDTH), lambda rb, bi: (bi, rb, 0)),
                  pl.BlockSpec((None, seq, NA_WIDTH), lambda rb, bi: (bi, 0, 1)),
                  pl.BlockSpec((None, seq, NA_WIDTH), lambda rb, bi: (bi, 0, 2)),
                  pl.BlockSpec((None, NA_HEADS, nq, nk), lambda rb, bi: (variant(rb), 0, 0, 0))],
        out_specs=pl.BlockSpec((None, nq, NA_WIDTH), lambda rb, bi: (bi, rb, 0)),
        out_shape=jax.ShapeDtypeStruct((b, seq, NA_WIDTH), BF16),
        compiler_params=_cparams("arbitrary", "arbitrary"),
        name="na",
    )(proj3, proj3, proj3, nb)


def _t5_thresholds():
    nb = T5_BUCKETS // 2
    me = nb // 2
    span = nb - me
    ratio = T5_MAX_DIST // me
    out = []
    for m in range(1, span):
        n = me
        while n ** span < me ** span * ratio ** m:
            n += 1
        out.append(n)
    return out


def _t5_bias_body(t5_ref, o_ref, *, nq):
    h = pl.program_id(0)
    nb = T5_BUCKETS // 2
    me = nb // 2
    thresholds = _t5_thresholds()
    assert DIFF_TILE >= thresholds[-1]
    block = pl.program_id(1) - (nq - 1)

    @pl.when(jnp.abs(block) <= 1)
    def _():
        q = lax.broadcasted_iota(jnp.int32, (DIFF_TILE, DIFF_TILE), 0)
        k = lax.broadcasted_iota(jnp.int32, (DIFF_TILE, DIFF_TILE), 1)
        rel = block * DIFF_TILE + k - q
        n = jnp.abs(rel)
        large = jnp.full(n.shape, me, jnp.int32)
        for thr in thresholds:
            large = large + jnp.where(n >= thr, 1, 0)
        bkt = jnp.where(rel > 0, nb, 0) + jnp.where(n < me, n, large)
        acc = jnp.zeros(bkt.shape, F32)
        for bb in range(T5_BUCKETS):
            acc = jnp.where(bkt == bb, t5_ref[bb * DIFF_HEADS + h], acc)
        o_ref[...] = acc

    @pl.when(jnp.abs(block) > 1)
    def _():
        far = jnp.where(block > 0, t5_ref[(T5_BUCKETS - 1) * DIFF_HEADS + h],
                        t5_ref[(nb - 1) * DIFF_HEADS + h])
        o_ref[...] = jnp.full(o_ref.shape, far, F32)


def _t5_bias(t5_flat, seq):
    nq = seq // DIFF_TILE
    nt = 2 * nq - 1
    return pl.pallas_call(
        functools.partial(_t5_bias_body, nq=nq),
        grid=(DIFF_HEADS, nt),
        in_specs=[pl.BlockSpec(memory_space=pltpu.SMEM)],
        out_specs=pl.BlockSpec((None, None, DIFF_TILE, DIFF_TILE), lambda h, t: (h, t, 0, 0)),
        out_shape=jax.ShapeDtypeStruct((DIFF_HEADS, nt, DIFF_TILE, DIFF_TILE), F32),
        compiler_params=_cparams("arbitrary", "arbitrary"),
        name="t5_bias",
    )(t5_flat)


def _diff_body(q_ref, k_ref, v_ref, dt_ref, lq1_ref, lk1_ref, lq2_ref, lk2_ref, g_ref, o_ref,
               *, nkb, lambda_init):
    qi = pl.program_id(2)
    q = q_ref[...]
    nsub = DIFF_TQ // DIFF_TILE
    lam = (jnp.exp(jnp.sum(lq1_ref[...] * lk1_ref[...], axis=-1, keepdims=True))
           - jnp.exp(jnp.sum(lq2_ref[...] * lk2_ref[...], axis=-1, keepdims=True)) + lambda_init)
    maps = []
    for m in _half_masks(DIFF_HEAD_DIM ** -0.5):
        qm = q * m
        pieces = []
        for kb in range(nkb):
            s = lax.dot_general(qm, k_ref[kb * DIFF_TILE:(kb + 1) * DIFF_TILE, :], _NT,
                                preferred_element_type=F32)
            bias = [dt_ref[nkb - 1 - (nsub * qi + hq) + kb] for hq in range(nsub)]
            pieces.append(s + jnp.concatenate(bias, axis=0))
        s = jnp.concatenate(pieces, axis=1)
        p = jnp.exp(s - jnp.max(s, axis=-1, keepdims=True))
        maps.append((p, jnp.sum(p, axis=-1, keepdims=True)))
    (p1, l1), (p2, l2) = maps
    w = p1 * (1.0 / l1) - p2 * (lam / l2)
    o = jnp.dot(w.astype(BF16), v_ref[...], preferred_element_type=F32)
    o = o * lax.rsqrt(jnp.mean(o * o, axis=-1, keepdims=True) + LN_EPS) * g_ref[...]
    o_ref[...] = (o * (1.0 - lambda_init)).astype(o_ref.dtype)


def _diff(proj3, dt, lq1, lk1, lq2, lk2, subln, lambda_init):
    b, seq, _ = proj3.shape
    nkb = seq // DIFF_TILE
    nt = 2 * nkb - 1
    c0 = 3 * NA_WIDTH // LANES
    vec = lambda n: pl.BlockSpec((1, n), lambda h, bi, qi: (0, 0))
    return pl.pallas_call(
        functools.partial(_diff_body, nkb=nkb, lambda_init=lambda_init),
        grid=(DIFF_HEADS, b, seq // DIFF_TQ),
        in_specs=[pl.BlockSpec((None, DIFF_TQ, LANES), lambda h, bi, qi: (bi, qi, c0 + h)),
                  pl.BlockSpec((None, seq, LANES), lambda h, bi, qi: (bi, 0, c0 + DIFF_HEADS + h)),
                  pl.BlockSpec((None, seq, LANES), lambda h, bi, qi: (bi, 0, c0 + 2 * DIFF_HEADS + h)),
                  pl.BlockSpec((None, nt, DIFF_TILE, DIFF_TILE), lambda h, bi, qi: (h, 0, 0, 0)),
                  vec(DIFF_HEAD_DIM), vec(DIFF_HEAD_DIM), vec(DIFF_HEAD_DIM), vec(DIFF_HEAD_DIM),
                  vec(DIFF_V_DIM)],
        out_specs=pl.BlockSpec((None, DIFF_TQ, LANES), lambda h, bi, qi: (bi, qi, h)),
        out_shape=jax.ShapeDtypeStruct((b, seq, DIFF_WIDTH), BF16),
        compiler_params=_cparams("arbitrary", "arbitrary", "arbitrary"),
        name="diff",
    )(proj3, proj3, proj3, dt, lq1, lk1, lq2, lk2, subln)


def _mid_body(ona_ref, odf_ref, wout_ref, x_ref, ga_ref, scf_ref, shf_ref, g1_ref, b1_ref,
              wq_ref, sk_ref, x1_ref, h2_ref, st_ref):
    mix = (jnp.dot(ona_ref[...], wout_ref[0:NA_WIDTH, :], preferred_element_type=F32)
           + jnp.dot(odf_ref[...], wout_ref[NA_WIDTH:, :], preferred_element_type=F32))
    y = DN_ALPHA * x_ref[...] + (1.0 + ga_ref[...]) * mix
    x1 = _ln(y) * g1_ref[...] + b1_ref[...]
    x1_ref[...] = x1
    h2 = (_ln(x1) * (1.0 + scf_ref[...]) + shf_ref[...]).astype(BF16)
    h2_ref[...] = h2
    qp = jnp.dot(h2, wq_ref[...], preferred_element_type=F32).astype(BF16)
    for hp in range(2 * PEER_HEADS):
        st_ref[hp] = lax.dot_general(sk_ref[hp % 2], qp[:, hp * PEER_NKEYS:(hp + 1) * PEER_NKEYS],
                                     _NT, preferred_element_type=F32)


def _mid(o_na, o_df, w_out, xf, mod4, ln_g, ln_b, w_query, sub_keys, seq):
    t, d = xf.shape
    tpb = seq // MID_TM
    nqk = w_query.shape[1]
    modspec = lambda j: pl.BlockSpec((None, None, 1, d), lambda i: (i // tpb, j, 0, 0))
    full = lambda *s: pl.BlockSpec(s, lambda i: (0,) * len(s))
    return pl.pallas_call(
        _mid_body,
        grid=(t // MID_TM,),
        in_specs=[pl.BlockSpec((MID_TM, NA_WIDTH), lambda i: (i, 0)),
                  pl.BlockSpec((MID_TM, DIFF_WIDTH), lambda i: (i, 0)),
                  full(d, d),
                  pl.BlockSpec((MID_TM, d), lambda i: (i, 0)),
                  modspec(2), modspec(4), modspec(3),
                  full(1, d), full(1, d),
                  full(d, nqk),
                  full(2, PEER_NKEYS, PEER_NKEYS)],
        out_specs=[pl.BlockSpec((MID_TM, d), lambda i: (i, 0)),
                   pl.BlockSpec((MID_TM, d), lambda i: (i, 0)),
                   pl.BlockSpec((2 * PEER_HEADS, PEER_NKEYS, MID_TM), lambda i: (0, 0, i))],
        out_shape=[jax.ShapeDtypeStruct((t, d), F32),
                   jax.ShapeDtypeStruct((t, d), BF16),
                   jax.ShapeDtypeStruct((2 * PEER_HEADS, PEER_NKEYS, t), F32)],
        compiler_params=_cparams("arbitrary"),
        name="mid",
    )(o_na, o_df, w_out, xf, mod4, mod4, mod4, ln_g, ln_b, w_query, sub_keys)


def _extract_top(work, order, tie_rule, out_ref=None, tok=None):
    rank = jnp.full(work.shape, float(PEER_TOPK), F32)
    big = float(PEER_N)
    for r in range(PEER_TOPK):
        m = jnp.max(work, axis=0, keepdims=True)
        sel = work == m
        if tie_rule:
            first = jnp.min(jnp.where(sel, order, big), axis=0, keepdims=True)
            sel = order == first
        rank = jnp.where(sel, float(r), rank)
        work = jnp.where(sel, -jnp.inf, work)
        if out_ref is not None:
            out_ref[r:r + 1, tok] = m
    return rank


def _column_max(x):
    slabs = [x[i:i + 8] for i in range(0, x.shape[0], 8)]
    while len(slabs) > 1:
        slabs = [jnp.maximum(slabs[i], slabs[i + 1]) if i + 1 < len(slabs) else slabs[i]
                 for i in range(0, len(slabs), 2)]
    return jnp.max(slabs[0], axis=0, keepdims=True)


def _pair_cells(op, u1, u2):
    g = [op(u1[0:1], u2[0:8]), op(u1[0:1], u2[8:16])]
    for a in range(1, 8):
        g.append(op(u1[a:a + 1], u2[0:8]))
    g.append(op(u1[8:16], u2[0:1]))
    return jnp.concatenate(g, axis=0)


def _topk_body(st_ref, lc_ref, f1_ref, r2_ref, f2_ref, t_ref, rk_ref, l_ref, z_ref, w2_ref):
    tl = st_ref.shape[-1]
    chunks = [slice(c * LANES, (c + 1) * LANES) for c in range(tl // LANES)]
    key_id = lax.broadcasted_iota(jnp.int32, (PEER_NKEYS, LANES), 0).astype(F32)

    def half_ranks(tie_rule):
        for tok in chunks:
            for p in range(2):
                rk_ref[p, :, tok] = _extract_top(st_ref[p, :, tok], key_id, tie_rule,
                                                 t_ref.at[p], tok)

    jobs = [(p, tok) for tok in chunks for p in range(2)]
    for j0 in range(0, len(jobs), FAST_CHAINS):
        group = jobs[j0:j0 + FAST_CHAINS]
        work = [st_ref[p, :, tok] for p, tok in group]
        for p, tok in group:
            rk_ref[p, :, tok] = jnp.full((PEER_NKEYS, LANES), float(PEER_TOPK), F32)
        for r in range(PEER_TOPK):
            for i, (p, tok) in enumerate(group):
                m = _column_max(work[i])
                sel = work[i] == m
                rk_ref[p, :, tok] = jnp.where(sel, float(r), rk_ref[p, :, tok])
                work[i] = jnp.where(sel, -jnp.inf, work[i])
                t_ref[p, r:r + 1, tok] = m
    taken = jnp.where(rk_ref[...] < float(PEER_TOPK), 1.0, 0.0)
    most = jnp.max(jnp.sum(taken, axis=1, keepdims=True))

    @pl.when(most > float(PEER_TOPK))
    def _():
        half_ranks(True)

    wide = 4 * LANES
    row = lax.broadcasted_iota(jnp.int32, (80, wide), 0)
    mid = row - 16
    cell_id = jnp.where(row < 16, row,
                        jnp.where(row < 72,
                                  (jnp.right_shift(mid, 3) + 1) * 16 + jnp.bitwise_and(mid, 7),
                                  (row - 64) * 16)).astype(F32)
    for c0 in range(0, tl, wide):
        tok2 = slice(c0, c0 + wide)
        t1 = t_ref[0, :, tok2]
        t2 = t_ref[1, :, tok2]
        work = _pair_cells(jnp.add, t1, t2)
        for _ in range(PEER_TOPK):
            work = jnp.where(work == _column_max(work), -jnp.inf, work)
        w2_ref[:, tok2] = work
        taken = jnp.sum(jnp.where(work == -jnp.inf, 1.0, 0.0), axis=0, keepdims=True)

        @pl.when(jnp.max(taken) > float(PEER_TOPK))
        def _():
            again = _pair_cells(jnp.add, t_ref[0, :, tok2], t_ref[1, :, tok2])
            for _ in range(PEER_TOPK):
                m = jnp.max(again, axis=0, keepdims=True)
                first = jnp.min(jnp.where(again == m, cell_id, float(PEER_N)), axis=0,
                                keepdims=True)
                again = jnp.where(cell_id == first, -jnp.inf, again)
            w2_ref[:, tok2] = again

        chosen = jnp.where(w2_ref[:, tok2] == -jnp.inf, 1.0, 0.0)
        e1 = jnp.exp(t1 - t1[0:1])
        e2 = jnp.exp(t2 - t2[0:1])
        z = jnp.sum(chosen * _pair_cells(jnp.multiply, e1, e2), axis=0, keepdims=True)
        z_ref[0:1, tok2] = 1.0 / z
        l_ref[0:1, tok2] = jnp.sum(chosen[0:16], axis=0, keepdims=True)
        for a in range(1, 8):
            l_ref[a:a + 1, tok2] = jnp.sum(chosen[8 + 8 * a:16 + 8 * a], axis=0, keepdims=True)
        l_ref[8:16, tok2] = chosen[72:80]
    for tok in chunks:
        rank1 = rk_ref[0, :, tok]
        rank2 = rk_ref[1, :, tok]
        lcnt = l_ref[:, tok]
        lc = jnp.zeros((PEER_NKEYS, LANES), F32)
        for a in range(PEER_TOPK):
            lc = jnp.where(rank1 == float(a), lcnt[a:a + 1], lc)
        lc_ref[:, tok] = lc
        f1_ref[:, tok] = jnp.exp(st_ref[0, :, tok] - t_ref[0, 0:1, tok]) * z_ref[0:1, tok]
        r2_ref[:, tok] = rank2
        f2_ref[:, tok] = jnp.where(rank2 < float(PEER_TOPK),
                                   jnp.exp(st_ref[1, :, tok] - t_ref[1, 0:1, tok]), 0.0)


def _topk(st):
    _, nk, t = st.shape
    tl = min(TOPK_TL, t)
    spec = pl.BlockSpec((None, nk, tl), lambda i, h: (h, 0, i))
    shp = jax.ShapeDtypeStruct((PEER_HEADS, nk, t), F32)
    return pl.pallas_call(
        _topk_body,
        grid=(t // tl, PEER_HEADS),
        in_specs=[pl.BlockSpec((2, nk, tl), lambda i, h: (h, 0, i))],
        out_specs=[spec, spec, spec, spec],
        out_shape=[shp, shp, shp, shp],
        scratch_shapes=[pltpu.VMEM((2, PEER_TOPK, tl), F32), pltpu.VMEM((2, nk, tl), F32),
                        pltpu.VMEM((PEER_TOPK, tl), F32), pltpu.VMEM((8, tl), F32),
                        pltpu.VMEM((80, tl), F32)],
        compiler_params=_cparams("arbitrary", "arbitrary"),
        name="topk",
    )(st)


def _peer_body(h2_ref, ed_ref, eut_ref, lc_ref, f1_ref, r2_ref, f2_ref, x1_ref, gf_ref,
               g2_ref, b2_ref, o_ref, acc_ref, *, ni):
    et = pl.program_id(1)

    @pl.when(et == 0)
    def _():
        acc_ref[...] = jnp.zeros(acc_ref.shape, F32)

    tm = h2_ref.shape[0]
    sub = BF16_SUBLANES
    kgroup = 2 * sub
    at = lax.dot_general(ed_ref[...], h2_ref[...], _NT, preferred_element_type=F32)
    cols = []
    for c in range(tm // LANES):
        tok = slice(c * LANES, (c + 1) * LANES)
        w_rows = {}
        for k0 in range(0, PEER_NKEYS, kgroup):
            subs = [slice(k0 + j * sub, k0 + (j + 1) * sub) for j in range(kgroup // sub)]
            rank = [[r2_ref[h, rk, tok].astype(BF16) for rk in subs] for h in range(PEER_HEADS)]
            val = [[f2_ref[h, rk, tok].astype(BF16) for rk in subs] for h in range(PEER_HEADS)]
            for ii in range(ni):
                g = [None] * len(subs)
                for h in range(PEER_HEADS):
                    cnt = jnp.broadcast_to(lc_ref[h, ii:ii + 1, tok], (sub, LANES)).astype(BF16)
                    fac = jnp.broadcast_to(f1_ref[h, ii:ii + 1, tok], (sub, LANES)).astype(BF16)
                    for j in range(len(subs)):
                        term = jnp.where(rank[h][j] < cnt, val[h][j], 0) * fac
                        g[j] = term if g[j] is None else g[j] + term
                for j, rk in enumerate(subs):
                    r0 = ii * PEER_NKEYS + rk.start
                    a = at[r0:r0 + sub, tok]
                    act = 0.5 * a * (1.0 + lax.erf(a * (2.0 ** -0.5)))
                    w_rows[r0] = g[j] * act.astype(BF16)
        cols.append(jnp.concatenate([w_rows[r0] for r0 in sorted(w_rows)], axis=0))
    wt = jnp.concatenate(cols, axis=1)
    acc_ref[...] += jnp.dot(eut_ref[...], wt, preferred_element_type=F32)

    @pl.when(et == pl.num_programs(1) - 1)
    def _():
        f = acc_ref[...].T
        y = DN_ALPHA * x1_ref[...] + (1.0 + gf_ref[...]) * f
        o_ref[...] = _ln(y) * g2_ref[...] + b2_ref[...]


def _peer(h2, ed, eut, lc, f1, r2, f2, x1, mod4, ln_g, ln_b, seq):
    t, d = h2.shape
    tm = min(PEER_TM, t)
    te = PEER_TE
    ni = te // PEER_NKEYS
    ne = PEER_N // te
    tpb = seq // tm
    lc4 = lc.reshape(PEER_HEADS, ne, ni, t)
    f14 = f1.reshape(PEER_HEADS, ne, ni, t)
    rowspec = pl.BlockSpec((PEER_HEADS, None, ni, tm), lambda i, e: (0, e, 0, i))
    colspec = pl.BlockSpec((PEER_HEADS, PEER_NKEYS, tm), lambda i, e: (0, 0, i))
    full = lambda *s: pl.BlockSpec(s, lambda i, e: (0,) * len(s))
    return pl.pallas_call(
        functools.partial(_peer_body, ni=ni),
        grid=(t // tm, ne),
        in_specs=[pl.BlockSpec((tm, d), lambda i, e: (i, 0)),
                  pl.BlockSpec((te, d), lambda i, e: (e, 0)),
                  pl.BlockSpec((d, te), lambda i, e: (0, e)),
                  rowspec, rowspec, colspec, colspec,
                  pl.BlockSpec((tm, d), lambda i, e: (i, 0)),
                  pl.BlockSpec((None, None, 1, d), lambda i, e: (i // tpb, 5, 0, 0)),
                  full(1, d), full(1, d)],
        out_specs=pl.BlockSpec((tm, d), lambda i, e: (i, 0)),
        out_shape=jax.ShapeDtypeStruct((t, d), F32),
        scratch_shapes=[pltpu.VMEM((d, tm), F32)],
        compiler_params=_cparams("arbitrary", "arbitrary"),
        name="peer",
    )(h2, ed, eut, lc4, f14, r2, f2, x1, mod4, ln_g, ln_b)


def kernel(x, c, w_ada, b_ada, w_in, w_out, na_rpb, t5_bias, lambda_q1, lambda_k1, lambda_q2,
           lambda_k2, diff_subln, ln1_g, ln1_b, w_query, sub_keys, expert_down, expert_up,
           ln2_g, ln2_b):
    b, seq, d = x.shape
    t = b * seq
    rows = seq // GRID_W
    assert d == D_MODEL and w_ada.shape[0] == DEPTH == 1
    assert rows % NA_QROWS == 0 and rows >= NA_KROWS and seq % DIFF_TQ == 0
    l = 0
    lambda_init = 0.8 - 0.6 * math.exp(-0.3 * l)
    xf = x.reshape(t, d)

    mod4 = _ada(c, w_ada[l], b_ada[l][None, :]).reshape(b, 6, 1, d)
    proj3 = _proj(xf, mod4, w_in[l].astype(BF16), seq).reshape(b, seq, IN_WIDTH)

    nb = _na_bias(na_rpb[l].reshape(-1), rows)
    o_na = _na(proj3, nb, rows).reshape(t, NA_WIDTH)

    dt = _t5_bias(t5_bias.reshape(-1), seq)
    o_df = _diff(proj3, dt, lambda_q1[l][None, :], lambda_k1[l][None, :], lambda_q2[l][None, :],
                 lambda_k2[l][None, :], diff_subln[l][None, :], lambda_init).reshape(t, DIFF_WIDTH)

    x1, h2, st = _mid(o_na, o_df, w_out[l].astype(BF16), xf, mod4, ln1_g[l][None, :],
                      ln1_b[l][None, :], w_query[l].astype(BF16), sub_keys[l].astype(BF16), seq)
    lc, f1, r2, f2 = _topk(st)
    out = _peer(h2, expert_down[l].astype(BF16), expert_up[l].astype(BF16).T, lc, f1, r2, f2,
                x1, mod4, ln2_g[l][None, :], ln2_b[l][None, :], seq)
    return out.reshape(b, seq, d)
```

```python
import functools
import math

import jax
import jax.numpy as jnp
from jax import lax
from jax.experimental import pallas as pl
from jax.experimental.pallas import tpu as pltpu

F32 = jnp.float32
BF16 = jnp.bfloat16

D_MODEL = 1024
DEPTH = 1
GRID_W = 64
NA_HEADS = 8
NA_HEAD_DIM = 64
NA_WIDTH = NA_HEADS * NA_HEAD_DIM
NA_WIN_ROWS = 8
NA_WIN_COLS = 16
DIFF_HEADS = 4
DIFF_HEAD_DIM = 64
DIFF_V_DIM = 2 * DIFF_HEAD_DIM
DIFF_WIDTH = DIFF_HEADS * DIFF_V_DIM
IN_WIDTH = 3 * NA_WIDTH + 3 * DIFF_WIDTH
T5_BUCKETS = 32
T5_MAX_DIST = 128
PEER_HEADS = 8
PEER_NKEYS = 128
PEER_N = PEER_NKEYS * PEER_NKEYS
PEER_TOPK = 16
LN_EPS = 1e-5
DN_ALPHA = (2.0 * DEPTH) ** 0.25
NEG_BIG = -1e30
LANES = 128
BF16_SUBLANES = 16

PROJ_TM = 512
NA_QROWS = 4
NA_KROWS = 12
DIFF_TQ = 256
DIFF_TILE = 256
MID_TM = 512
TOPK_TL = 1024
FAST_CHAINS = 4
PEER_TM = 512
PEER_TE = 2048
VMEM_LIMIT = 56 * 1024 * 1024

_NT = (((1,), (1,)), ((), ()))


def _cparams(*sem):
    return pltpu.CompilerParams(dimension_semantics=sem, vmem_limit_bytes=VMEM_LIMIT)


def _ln(x):
    mu = jnp.mean(x, axis=-1, keepdims=True)
    xc = x - mu
    var = jnp.mean(xc * xc, axis=-1, keepdims=True)
    return xc * lax.rsqrt(var + LN_EPS)


def _ada_body(c_ref, w_ref, b_ref, o_ref):
    c = c_ref[...]
    s = c * jax.nn.sigmoid(c)
    o_ref[...] = jnp.dot(s, w_ref[...], preferred_element_type=F32) + b_ref[...]


def _ada(c, w, bias):
    b, d = c.shape
    n = w.shape[1]
    return pl.pallas_call(
        _ada_body,
        grid=(n // d,),
        in_specs=[pl.BlockSpec((b, d), lambda j: (0, 0)),
                  pl.BlockSpec((d, d), lambda j: (0, j)),
                  pl.BlockSpec((1, d), lambda j: (0, j))],
        out_specs=pl.BlockSpec((b, d), lambda j: (0, j)),
        out_shape=jax.ShapeDtypeStruct((b, n), F32),
        compiler_params=_cparams("arbitrary"),
        name="ada",
    )(c, w, bias)


def _proj_body(x_ref, sc_ref, sh_ref, w_ref, o_ref):
    h = _ln(x_ref[...]) * (1.0 + sc_ref[...]) + sh_ref[...]
    o_ref[...] = jnp.dot(h.astype(BF16), w_ref[...], preferred_element_type=F32).astype(o_ref.dtype)


def _proj(xf, mod4, w, seq):
    t, d = xf.shape
    n = w.shape[1]
    tpb = seq // PROJ_TM
    return pl.pallas_call(
        _proj_body,
        grid=(t // PROJ_TM,),
        in_specs=[pl.BlockSpec((PROJ_TM, d), lambda i: (i, 0)),
                  pl.BlockSpec((None, None, 1, d), lambda i: (i // tpb, 1, 0, 0)),
                  pl.BlockSpec((None, None, 1, d), lambda i: (i // tpb, 0, 0, 0)),
                  pl.BlockSpec((d, n), lambda i: (0, 0))],
        out_specs=pl.BlockSpec((PROJ_TM, n), lambda i: (i, 0)),
        out_shape=jax.ShapeDtypeStruct((t, n), BF16),
        compiler_params=_cparams("arbitrary"),
        name="proj",
    )(xf, mod4, mod4, w)


def _na_row_offsets(rows):
    kr = min(NA_WIN_ROWS, rows)
    nrb = rows // NA_QROWS
    pats = []
    for rb in range(nrb):
        r0 = rb * NA_QROWS
        s = min(max(r0 - NA_WIN_ROWS // 2, 0), rows - NA_KROWS)
        pat = []
        for qr in range(NA_QROWS):
            r = r0 + qr
            rs = min(max(r - kr // 2, 0), rows - kr)
            pat.append(tuple((s + i) - r + (NA_WIN_ROWS - 1) if rs <= s + i < rs + kr else 15
                             for i in range(NA_KROWS)))
        pats.append(tuple(pat))
    variants = [pats[0], pats[1], pats[-1]]
    assert all(p == variants[1] for p in pats[1:-1])
    return variants


def _na_bias_body(rpb_ref, o_ref, tile_ref, *, variants):
    h = pl.program_id(0)
    nro = 2 * NA_WIN_ROWS - 1
    nco = 2 * NA_WIN_COLS - 1
    q = lax.broadcasted_iota(jnp.int32, (GRID_W, LANES), 0)
    kk = lax.broadcasted_iota(jnp.int32, (GRID_W, LANES), 1)
    k = jnp.bitwise_and(kk, GRID_W - 1)
    col_start = jnp.clip(q - NA_WIN_COLS // 2, 0, GRID_W - NA_WIN_COLS)
    valid = jnp.logical_and(k >= col_start, k < col_start + NA_WIN_COLS)
    co = jnp.clip(k - q, -(NA_WIN_COLS - 1), NA_WIN_COLS - 1) + (NA_WIN_COLS - 1)
    neg = jnp.full((GRID_W, LANES), NEG_BIG, F32)
    for ro in range(nro):
        acc = neg
        for cc in range(nco):
            acc = jnp.where(co == cc, rpb_ref[(h * nro + ro) * nco + cc], acc)
        tile_ref[ro] = jnp.where(valid, acc, NEG_BIG)
    tile_ref[nro] = neg
    left = kk < GRID_W
    for v, pat in enumerate(variants):
        for qr in range(NA_QROWS):
            for kp in range(NA_KROWS // 2):
                a, b = pat[qr][2 * kp], pat[qr][2 * kp + 1]
                o_ref[v, qr * GRID_W:(qr + 1) * GRID_W, kp * LANES:(kp + 1) * LANES] = (
                    jnp.where(left, tile_ref[a], tile_ref[b]))


def _na_bias(rpb_flat, rows):
    variants = _na_row_offsets(rows)
    nq, nk = NA_QROWS * GRID_W, NA_KROWS * GRID_W
    return pl.pallas_call(
        functools.partial(_na_bias_body, variants=variants),
        grid=(NA_HEADS,),
        in_specs=[pl.BlockSpec(memory_space=pltpu.SMEM)],
        out_specs=pl.BlockSpec((3, None, nq, nk), lambda h: (0, h, 0, 0)),
        out_shape=jax.ShapeDtypeStruct((3, NA_HEADS, nq, nk), F32),
        scratch_shapes=[pltpu.VMEM((2 * NA_WIN_ROWS, GRID_W, LANES), F32)],
        compiler_params=_cparams("arbitrary"),
        name="na_bias",
    )(rpb_flat)


def _half_masks(value=1.0):
    lane = lax.broadcasted_iota(jnp.int32, (1, LANES), 1)
    lo = jnp.where(lane < LANES // 2, value, 0.0).astype(BF16)
    hi = jnp.where(lane < LANES // 2, 0.0, value).astype(BF16)
    return lo, hi


def _na_body(q_ref, k_ref, v_ref, b_ref, o_ref, *, rows):
    rb = pl.program_id(0)
    s_row = jnp.clip(rb * NA_QROWS - NA_WIN_ROWS // 2, 0, rows - NA_KROWS)
    start = pl.multiple_of(s_row * GRID_W, GRID_W)
    nk = NA_KROWS * GRID_W
    masks = list(zip(_half_masks(NA_HEAD_DIM ** -0.5), _half_masks()))
    for hp in range(NA_HEADS // 2):
        cols = slice(hp * LANES, (hp + 1) * LANES)
        q = q_ref[:, cols]
        kw = k_ref[pl.ds(start, nk), cols]
        vw = v_ref[pl.ds(start, nk), cols]
        acc = jnp.zeros(q.shape, F32)
        for h, (mq, mv) in enumerate(masks):
            s = lax.dot_general(q * mq, kw, _NT, preferred_element_type=F32) + b_ref[2 * hp + h]
            p = jnp.exp(s - jnp.max(s, axis=-1, keepdims=True))
            l = jnp.sum(p, axis=-1, keepdims=True)
            o = jnp.dot(p.astype(BF16), vw * mv, preferred_element_type=F32)
            acc = acc + o * (1.0 / l)
        o_ref[:, cols] = acc.astype(o_ref.dtype)


def _na(proj3, nb, rows):
    b, seq, _ = proj3.shape
    nrb = rows // NA_QROWS
    nq, nk = NA_QROWS * GRID_W, NA_KROWS * GRID_W

    def variant(rb):
        return jnp.where(rb == 0, 0, jnp.where(rb == nrb - 1, 2, 1))

    return pl.pallas_call(
        functools.partial(_na_body, rows=rows),
        grid=(nrb, b),
        in_specs=[pl.BlockSpec((None, nq, NA_WIDTH), lambda rb, bi: (bi, rb, 0)),
                  pl.BlockSpec((None, seq, NA_WIDTH), lambda rb, bi: (bi, 0, 1)),
                  pl.BlockSpec((None, seq, NA_WIDTH), lambda rb, bi: (bi, 0, 2)),
                  pl.BlockSpec((None, NA_HEADS, nq, nk), lambda rb, bi: (variant(rb), 0, 0, 0))],
        out_specs=pl.BlockSpec((None, nq, NA_WIDTH), lambda rb, bi: (bi, rb, 0)),
        out_shape=jax.ShapeDtypeStruct((b, seq, NA_WIDTH), BF16),
        compiler_params=_cparams("arbitrary", "arbitrary"),
        name="na",
    )(proj3, proj3, proj3, nb)


def _t5_thresholds():
    nb = T5_BUCKETS // 2
    me = nb // 2
    span = nb - me
    ratio = T5_MAX_DIST // me
    out = []
    for m in range(1, span):
        n = me
        while n ** span < me ** span * ratio ** m:
            n += 1
        out.append(n)
    return out


def _t5_bias_body(t5_ref, o_ref, *, nq):
    h = pl.program_id(0)
    nb = T5_BUCKETS // 2
    me = nb // 2
    thresholds = _t5_thresholds()
    assert DIFF_TILE >= thresholds[-1]
    block = pl.program_id(1) - (nq - 1)

    @pl.when(jnp.abs(block) <= 1)
    def _():
        q = lax.broadcasted_iota(jnp.int32, (DIFF_TILE, DIFF_TILE), 0)
        k = lax.broadcasted_iota(jnp.int32, (DIFF_TILE, DIFF_TILE), 1)
        rel = block * DIFF_TILE + k - q
        n = jnp.abs(rel)
        large = jnp.full(n.shape, me, jnp.int32)
        for thr in thresholds:
            large = large + jnp.where(n >= thr, 1, 0)
        bkt = jnp.where(rel > 0, nb, 0) + jnp.where(n < me, n, large)
        acc = jnp.zeros(bkt.shape, F32)
        for bb in range(T5_BUCKETS):
            acc = jnp.where(bkt == bb, t5_ref[bb * DIFF_HEADS + h], acc)
        o_ref[...] = acc

    @pl.when(jnp.abs(block) > 1)
    def _():
        far = jnp.where(block > 0, t5_ref[(T5_BUCKETS - 1) * DIFF_HEADS + h],
                        t5_ref[(nb - 1) * DIFF_HEADS + h])
        o_ref[...] = jnp.full(o_ref.shape, far, F32)


def _t5_bias(t5_flat, seq):
    nq = seq // DIFF_TILE
    nt = 2 * nq - 1
    return pl.pallas_call(
        functools.partial(_t5_bias_body, nq=nq),
        grid=(DIFF_HEADS, nt),
        in_specs=[pl.BlockSpec(memory_space=pltpu.SMEM)],
        out_specs=pl.BlockSpec((None, None, DIFF_TILE, DIFF_TILE), lambda h, t: (h, t, 0, 0)),
        out_shape=jax.ShapeDtypeStruct((DIFF_HEADS, nt, DIFF_TILE, DIFF_TILE), F32),
        compiler_params=_cparams("arbitrary", "arbitrary"),
        name="t5_bias",
    )(t5_flat)


def _diff_body(q_ref, k_ref, v_ref, dt_ref, lq1_ref, lk1_ref, lq2_ref, lk2_ref, g_ref, o_ref,
               *, nkb, lambda_init):
    qi = pl.program_id(2)
    q = q_ref[...]
    nsub = DIFF_TQ // DIFF_TILE
    lam = (jnp.exp(jnp.sum(lq1_ref[...] * lk1_ref[...], axis=-1, keepdims=True))
           - jnp.exp(jnp.sum(lq2_ref[...] * lk2_ref[...], axis=-1, keepdims=True)) + lambda_init)
    maps = []
    for m in _half_masks(DIFF_HEAD_DIM ** -0.5):
        qm = q * m
        pieces = []
        for kb in range(nkb):
            s = lax.dot_general(qm, k_ref[kb * DIFF_TILE:(kb + 1) * DIFF_TILE, :], _NT,
                                preferred_element_type=F32)
            bias = [dt_ref[nkb - 1 - (nsub * qi + hq) + kb] for hq in range(nsub)]
            pieces.append(s + jnp.concatenate(bias, axis=0))
        s = jnp.concatenate(pieces, axis=1)
        p = jnp.exp(s - jnp.max(s, axis=-1, keepdims=True))
        maps.append((p, jnp.sum(p, axis=-1, keepdims=True)))
    (p1, l1), (p2, l2) = maps
    w = p1 * (1.0 / l1) - p2 * (lam / l2)
    o = jnp.dot(w.astype(BF16), v_ref[...], preferred_element_type=F32)
    o = o * lax.rsqrt(jnp.mean(o * o, axis=-1, keepdims=True) + LN_EPS) * g_ref[...]
    o_ref[...] = (o * (1.0 - lambda_init)).astype(o_ref.dtype)


def _diff(proj3, dt, lq1, lk1, lq2, lk2, subln, lambda_init):
    b, seq, _ = proj3.shape
    nkb = seq // DIFF_TILE
    nt = 2 * nkb - 1
    c0 = 3 * NA_WIDTH // LANES
    vec = lambda n: pl.BlockSpec((1, n), lambda h, bi, qi: (0, 0))
    return pl.pallas_call(
        functools.partial(_diff_body, nkb=nkb, lambda_init=lambda_init),
        grid=(DIFF_HEADS, b, seq // DIFF_TQ),
        in_specs=[pl.BlockSpec((None, DIFF_TQ, LANES), lambda h, bi, qi: (bi, qi, c0 + h)),
                  pl.BlockSpec((None, seq, LANES), lambda h, bi, qi: (bi, 0, c0 + DIFF_HEADS + h)),
                  pl.BlockSpec((None, seq, LANES), lambda h, bi, qi: (bi, 0, c0 + 2 * DIFF_HEADS + h)),
                  pl.BlockSpec((None, nt, DIFF_TILE, DIFF_TILE), lambda h, bi, qi: (h, 0, 0, 0)),
                  vec(DIFF_HEAD_DIM), vec(DIFF_HEAD_DIM), vec(DIFF_HEAD_DIM), vec(DIFF_HEAD_DIM),
                  vec(DIFF_V_DIM)],
        out_specs=pl.BlockSpec((None, DIFF_TQ, LANES), lambda h, bi, qi: (bi, qi, h)),
        out_shape=jax.ShapeDtypeStruct((b, seq, DIFF_WIDTH), BF16),
        compiler_params=_cparams("arbitrary", "arbitrary", "arbitrary"),
        name="diff",
    )(proj3, proj3, proj3, dt, lq1, lk1, lq2, lk2, subln)


def _mid_body(ona_ref, odf_ref, wout_ref, x_ref, ga_ref, scf_ref, shf_ref, g1_ref, b1_ref,
              wq_ref, sk_ref, x1_ref, h2_ref, st_ref):
    mix = (jnp.dot(ona_ref[...], wout_ref[0:NA_WIDTH, :], preferred_element_type=F32)
           + jnp.dot(odf_ref[...], wout_ref[NA_WIDTH:, :], preferred_element_type=F32))
    y = DN_ALPHA * x_ref[...] + (1.0 + ga_ref[...]) * mix
    x1 = _ln(y) * g1_ref[...] + b1_ref[...]
    x1_ref[...] = x1
    h2 = (_ln(x1) * (1.0 + scf_ref[...]) + shf_ref[...]).astype(BF16)
    h2_ref[...] = h2
    qp = jnp.dot(h2, wq_ref[...], preferred_element_type=F32).astype(BF16)
    for hp in range(2 * PEER_HEADS):
        st_ref[hp] = lax.dot_general(sk_ref[hp % 2], qp[:, hp * PEER_NKEYS:(hp + 1) * PEER_NKEYS],
                                     _NT, preferred_element_type=F32)


def _mid(o_na, o_df, w_out, xf, mod4, ln_g, ln_b, w_query, sub_keys, seq):
    t, d = xf.shape
    tpb = seq // MID_TM
    nqk = w_query.shape[1]
    modspec = lambda j: pl.BlockSpec((None, None, 1, d), lambda i: (i // tpb, j, 0, 0))
    full = lambda *s: pl.BlockSpec(s, lambda i: (0,) * len(s))
    return pl.pallas_call(
        _mid_body,
        grid=(t // MID_TM,),
        in_specs=[pl.BlockSpec((MID_TM, NA_WIDTH), lambda i: (i, 0)),
                  pl.BlockSpec((MID_TM, DIFF_WIDTH), lambda i: (i, 0)),
                  full(d, d),
                  pl.BlockSpec((MID_TM, d), lambda i: (i, 0)),
                  modspec(2), modspec(4), modspec(3),
                  full(1, d), full(1, d),
                  full(d, nqk),
                  full(2, PEER_NKEYS, PEER_NKEYS)],
        out_specs=[pl.BlockSpec((MID_TM, d), lambda i: (i, 0)),
                   pl.BlockSpec((MID_TM, d), lambda i: (i, 0)),
                   pl.BlockSpec((2 * PEER_HEADS, PEER_NKEYS, MID_TM), lambda i: (0, 0, i))],
        out_shape=[jax.ShapeDtypeStruct((t, d), F32),
                   jax.ShapeDtypeStruct((t, d), BF16),
                   jax.ShapeDtypeStruct((2 * PEER_HEADS, PEER_NKEYS, t), F32)],
        compiler_params=_cparams("arbitrary"),
        name="mid",
    )(o_na, o_df, w_out, xf, mod4, mod4, mod4, ln_g, ln_b, w_query, sub_keys)


def _extract_top(work, order, tie_rule, out_ref=None, tok=None):
    rank = jnp.full(work.shape, float(PEER_TOPK), F32)
    big = float(PEER_N)
    for r in range(PEER_TOPK):
        m = jnp.max(work, axis=0, keepdims=True)
        sel = work == m
        if tie_rule:
            first = jnp.min(jnp.where(sel, order, big), axis=0, keepdims=True)
            sel = order == first
        rank = jnp.where(sel, float(r), rank)
        work = jnp.where(sel, -jnp.inf, work)
        if out_ref is not None:
            out_ref[r:r + 1, tok] = m
    return rank


def _column_max(x):
    slabs = [x[i:i + 8] for i in range(0, x.shape[0], 8)]
    while len(slabs) > 1:
        slabs = [jnp.maximum(slabs[i], slabs[i + 1]) if i + 1 < len(slabs) else slabs[i]
                 for i in range(0, len(slabs), 2)]
    return jnp.max(slabs[0], axis=0, keepdims=True)


def _pair_cells(op, u1, u2):
    g = [op(u1[0:1], u2[0:8]), op(u1[0:1], u2[8:16])]
    for a in range(1, 8):
        g.append(op(u1[a:a + 1], u2[0:8]))
    g.append(op(u1[8:16], u2[0:1]))
    return jnp.concatenate(g, axis=0)


def _topk_body(st_ref, lc_ref, f1_ref, r2_ref, f2_ref, t_ref, rk_ref, l_ref, z_ref, w2_ref):
    tl = st_ref.shape[-1]
    chunks = [slice(c * LANES, (c + 1) * LANES) for c in range(tl // LANES)]
    key_id = lax.broadcasted_iota(jnp.int32, (PEER_NKEYS, LANES), 0).astype(F32)

    def half_ranks(tie_rule):
        for tok in chunks:
            for p in range(2):
                rk_ref[p, :, tok] = _extract_top(st_ref[p, :, tok], key_id, tie_rule,
                                                 t_ref.at[p], tok)

    jobs = [(p, tok) for tok in chunks for p in range(2)]
    for j0 in range(0, len(jobs), FAST_CHAINS):
        group = jobs[j0:j0 + FAST_CHAINS]
        work = [st_ref[p, :, tok] for p, tok in group]
        for p, tok in group:
            rk_ref[p, :, tok] = jnp.full((PEER_NKEYS, LANES), float(PEER_TOPK), F32)
        for r in range(PEER_TOPK):
            for i, (p, tok) in enumerate(group):
                m = _column_max(work[i])
                sel = work[i] == m
                rk_ref[p, :, tok] = jnp.where(sel, float(r), rk_ref[p, :, tok])
                work[i] = jnp.where(sel, -jnp.inf, work[i])
                t_ref[p, r:r + 1, tok] = m
    taken = jnp.where(rk_ref[...] < float(PEER_TOPK), 1.0, 0.0)
    most = jnp.max(jnp.sum(taken, axis=1, keepdims=True))

    @pl.when(most > float(PEER_TOPK))
    def _():
        half_ranks(True)

    wide = 4 * LANES
    row = lax.broadcasted_iota(jnp.int32, (80, wide), 0)
    mid = row - 16
    cell_id = jnp.where(row < 16, row,
                        jnp.where(row < 72,
                                  (jnp.right_shift(mid, 3) + 1) * 16 + jnp.bitwise_and(mid, 7),
                                  (row - 64) * 16)).astype(F32)
    for c0 in range(0, tl, wide):
        tok2 = slice(c0, c0 + wide)
        t1 = t_ref[0, :, tok2]
        t2 = t_ref[1, :, tok2]
        work = _pair_cells(jnp.add, t1, t2)
        for _ in range(PEER_TOPK):
            work = jnp.where(work == _column_max(work), -jnp.inf, work)
        w2_ref[:, tok2] = work
        taken = jnp.sum(jnp.where(work == -jnp.inf, 1.0, 0.0), axis=0, keepdims=True)

        @pl.when(jnp.max(taken) > float(PEER_TOPK))
        def _():
            again = _pair_cells(jnp.add, t_ref[0, :, tok2], t_ref[1, :, tok2])
            for _ in range(PEER_TOPK):
                m = jnp.max(again, axis=0, keepdims=True)
                first = jnp.min(jnp.where(again == m, cell_id, float(PEER_N)), axis=0,
                                keepdims=True)
                again = jnp.where(cell_id == first, -jnp.inf, again)
            w2_ref[:, tok2] = again

        chosen = jnp.where(w2_ref[:, tok2] == -jnp.inf, 1.0, 0.0)
        e1 = jnp.exp(t1 - t1[0:1])
        e2 = jnp.exp(t2 - t2[0:1])
        z = jnp.sum(chosen * _pair_cells(jnp.multiply, e1, e2), axis=0, keepdims=True)
        z_ref[0:1, tok2] = 1.0 / z
        l_ref[0:1, tok2] = jnp.sum(chosen[0:16], axis=0, keepdims=True)
        for a in range(1, 8):
            l_ref[a:a + 1, tok2] = jnp.sum(chosen[8 + 8 * a:16 + 8 * a], axis=0, keepdims=True)
        l_ref[8:16, tok2] = chosen[72:80]
    for tok in chunks:
        rank1 = rk_ref[0, :, tok]
        rank2 = rk_ref[1, :, tok]
        lcnt = l_ref[:, tok]
        lc = jnp.zeros((PEER_NKEYS, LANES), F32)
        for a in range(PEER_TOPK):
            lc = jnp.where(rank1 == float(a), lcnt[a:a + 1], lc)
        lc_ref[:, tok] = lc
        f1_ref[:, tok] = jnp.exp(st_ref[0, :, tok] - t_ref[0, 0:1, tok]) * z_ref[0:1, tok]
        r2_ref[:, tok] = rank2
        f2_ref[:, tok] = jnp.where(rank2 < float(PEER_TOPK),
                                   jnp.exp(st_ref[1, :, tok] - t_ref[1, 0:1, tok]), 0.0)


def _topk(st):
    _, nk, t = st.shape
    tl = min(TOPK_TL, t)
    spec = pl.BlockSpec((None, nk, tl), lambda i, h: (h, 0, i))
    shp = jax.ShapeDtypeStruct((PEER_HEADS, nk, t), F32)
    return pl.pallas_call(
        _topk_body,
        grid=(t // tl, PEER_HEADS),
        in_specs=[pl.BlockSpec((2, nk, tl), lambda i, h: (h, 0, i))],
        out_specs=[spec, spec, spec, spec],
        out_shape=[shp, shp, shp, shp],
        scratch_shapes=[pltpu.VMEM((2, PEER_TOPK, tl), F32), pltpu.VMEM((2, nk, tl), F32),
                        pltpu.VMEM((PEER_TOPK, tl), F32), pltpu.VMEM((8, tl), F32),
                        pltpu.VMEM((80, tl), F32)],
        compiler_params=_cparams("arbitrary", "arbitrary"),
        name="topk",
    )(st)


def _peer_body(h2_ref, ed_ref, eut_ref, lc_ref, f1_ref, r2_ref, f2_ref, x1_ref, gf_ref,
               g2_ref, b2_ref, o_ref, acc_ref, *, ni):
    et = pl.program_id(1)

    @pl.when(et == 0)
    def _():
        acc_ref[...] = jnp.zeros(acc_ref.shape, F32)

    tm = h2_ref.shape[0]
    sub = BF16_SUBLANES
    kgroup = 2 * sub
    at = lax.dot_general(ed_ref[...], h2_ref[...], _NT, preferred_element_type=F32)
    cols = []
    for c in range(tm // LANES):
        tok = slice(c * LANES, (c + 1) * LANES)
        w_rows = {}
        for k0 in range(0, PEER_NKEYS, kgroup):
            subs = [slice(k0 + j * sub, k0 + (j + 1) * sub) for j in range(kgroup // sub)]
            rank = [[r2_ref[h, rk, tok].astype(BF16) for rk in subs] for h in range(PEER_HEADS)]
            val = [[f2_ref[h, rk, tok].astype(BF16) for rk in subs] for h in range(PEER_HEADS)]
            for ii in range(ni):
                g = [None] * len(subs)
                for h in range(PEER_HEADS):
                    cnt = jnp.broadcast_to(lc_ref[h, ii:ii + 1, tok], (sub, LANES)).astype(BF16)
                    fac = jnp.broadcast_to(f1_ref[h, ii:ii + 1, tok], (sub, LANES)).astype(BF16)
                    for j in range(len(subs)):
                        term = jnp.where(rank[h][j] < cnt, val[h][j], 0) * fac
                        g[j] = term if g[j] is None else g[j] + term
                for j, rk in enumerate(subs):
                    r0 = ii * PEER_NKEYS + rk.start
                    a = at[r0:r0 + sub, tok]
                    act = 0.5 * a * (1.0 + lax.erf(a * (2.0 ** -0.5)))
                    w_rows[r0] = g[j] * act.astype(BF16)
        cols.append(jnp.concatenate([w_rows[r0] for r0 in sorted(w_rows)], axis=0))
    wt = jnp.concatenate(cols, axis=1)
    acc_ref[...] += lax.dot_general(eut_ref[...], wt, (((0,), (0,)), ((), ())),
                                    preferred_element_type=F32)

    @pl.when(et == pl.num_programs(1) - 1)
    def _():
        f = acc_ref[...].T
        y = DN_ALPHA * x1_ref[...] + (1.0 + gf_ref[...]) * f
        o_ref[...] = _ln(y) * g2_ref[...] + b2_ref[...]


def _peer(h2, ed, eut, lc, f1, r2, f2, x1, mod4, ln_g, ln_b, seq):
    t, d = h2.shape
    tm = min(PEER_TM, t)
    te = PEER_TE
    ni = te // PEER_NKEYS
    ne = PEER_N // te
    tpb = seq // tm
    lc4 = lc.reshape(PEER_HEADS, ne, ni, t)
    f14 = f1.reshape(PEER_HEADS, ne, ni, t)
    rowspec = pl.BlockSpec((PEER_HEADS, None, ni, tm), lambda i, e: (0, e, 0, i))
    colspec = pl.BlockSpec((PEER_HEADS, PEER_NKEYS, tm), lambda i, e: (0, 0, i))
    full = lambda *s: pl.BlockSpec(s, lambda i, e: (0,) * len(s))
    return pl.pallas_call(
        functools.partial(_peer_body, ni=ni),
        grid=(t // tm, ne),
        in_specs=[pl.BlockSpec((tm, d), lambda i, e: (i, 0)),
                  pl.BlockSpec((te, d), lambda i, e: (e, 0)),
                  pl.BlockSpec((te, d), lambda i, e: (e, 0)),
                  rowspec, rowspec, colspec, colspec,
                  pl.BlockSpec((tm, d), lambda i, e: (i, 0)),
                  pl.BlockSpec((None, None, 1, d), lambda i, e: (i // tpb, 5, 0, 0)),
                  full(1, d), full(1, d)],
        out_specs=pl.BlockSpec((tm, d), lambda i, e: (i, 0)),
        out_shape=jax.ShapeDtypeStruct((t, d), F32),
        scratch_shapes=[pltpu.VMEM((d, tm), F32)],
        compiler_params=_cparams("arbitrary", "arbitrary"),
        name="peer",
    )(h2, ed, eut, lc4, f14, r2, f2, x1, mod4, ln_g, ln_b)


def kernel(x, c, w_ada, b_ada, w_in, w_out, na_rpb, t5_bias, lambda_q1, lambda_k1, lambda_q2,
           lambda_k2, diff_subln, ln1_g, ln1_b, w_query, sub_keys, expert_down, expert_up,
           ln2_g, ln2_b):
    b, seq, d = x.shape
    t = b * seq
    rows = seq // GRID_W
    assert d == D_MODEL and w_ada.shape[0] == DEPTH == 1
    assert rows % NA_QROWS == 0 and rows >= NA_KROWS and seq % DIFF_TQ == 0
    l = 0
    lambda_init = 0.8 - 0.6 * math.exp(-0.3 * l)
    xf = x.reshape(t, d)

    mod4 = _ada(c, w_ada[l], b_ada[l][None, :]).reshape(b, 6, 1, d)
    proj3 = _proj(xf, mod4, w_in[l].astype(BF16), seq).reshape(b, seq, IN_WIDTH)

    nb = _na_bias(na_rpb[l].reshape(-1), rows)
    o_na = _na(proj3, nb, rows).reshape(t, NA_WIDTH)

    dt = _t5_bias(t5_bias.reshape(-1), seq)
    o_df = _diff(proj3, dt, lambda_q1[l][None, :], lambda_k1[l][None, :], lambda_q2[l][None, :],
                 lambda_k2[l][None, :], diff_subln[l][None, :], lambda_init).reshape(t, DIFF_WIDTH)

    x1, h2, st = _mid(o_na, o_df, w_out[l].astype(BF16), xf, mod4, ln1_g[l][None, :],
                      ln1_b[l][None, :], w_query[l].astype(BF16), sub_keys[l].astype(BF16), seq)
    lc, f1, r2, f2 = _topk(st)
    out = _peer(h2, expert_down[l].astype(BF16), expert_up[l].astype(BF16), lc, f1, r2, f2,
                x1, mod4, ln2_g[l][None, :], ln2_b[l][None, :], seq)
    return out.reshape(b, seq, d)
```

```python
import functools
import math

import jax
import jax.numpy as jnp
from jax import lax
from jax.experimental import pallas as pl
from jax.experimental.pallas import tpu as pltpu

F32 = jnp.float32
BF16 = jnp.bfloat16

D_MODEL = 1024
DEPTH = 1
GRID_W = 64
NA_HEADS = 8
NA_HEAD_DIM = 64
NA_WIDTH = NA_HEADS * NA_HEAD_DIM
NA_WIN_ROWS = 8
NA_WIN_COLS = 16
DIFF_HEADS = 4
DIFF_HEAD_DIM = 64
DIFF_V_DIM = 2 * DIFF_HEAD_DIM
DIFF_WIDTH = DIFF_HEADS * DIFF_V_DIM
IN_WIDTH = 3 * NA_WIDTH + 3 * DIFF_WIDTH
T5_BUCKETS = 32
T5_MAX_DIST = 128
PEER_HEADS = 8
PEER_NKEYS = 128
PEER_N = PEER_NKEYS * PEER_NKEYS
PEER_TOPK = 16
LN_EPS = 1e-5
DN_ALPHA = (2.0 * DEPTH) ** 0.25
NEG_BIG = -1e30
LANES = 128
BF16_SUBLANES = 16

PROJ_TM = 512
NA_QROWS = 4
NA_KROWS = 12
DIFF_TQ = 256
DIFF_TILE = 256
MID_TM = 512
TOPK_TL = 1024
FAST_CHAINS = 4
PEER_TM = 512
PEER_TE = 2048
VMEM_LIMIT = 56 * 1024 * 1024

_NT = (((1,), (1,)), ((), ()))


def _cparams(*sem):
    return pltpu.CompilerParams(dimension_semantics=sem, vmem_limit_bytes=VMEM_LIMIT)


def _ln(x):
    mu = jnp.mean(x, axis=-1, keepdims=True)
    xc = x - mu
    var = jnp.mean(xc * xc, axis=-1, keepdims=True)
    return xc * lax.rsqrt(var + LN_EPS)


def _ada_body(c_ref, w_ref, b_ref, o_ref):
    c = c_ref[...]
    s = c * jax.nn.sigmoid(c)
    o_ref[...] = jnp.dot(s, w_ref[...], preferred_element_type=F32) + b_ref[...]


def _ada(c, w, bias):
    b, d = c.shape
    n = w.shape[1]
    return pl.pallas_call(
        _ada_body,
        grid=(n // d,),
        in_specs=[pl.BlockSpec((b, d), lambda j: (0, 0)),
                  pl.BlockSpec((d, d), lambda j: (0, j)),
                  pl.BlockSpec((1, d), lambda j: (0, j))],
        out_specs=pl.BlockSpec((b, d), lambda j: (0, j)),
        out_shape=jax.ShapeDtypeStruct((b, n), F32),
        compiler_params=_cparams("arbitrary"),
        name="ada",
    )(c, w, bias)


def _proj_body(x_ref, sc_ref, sh_ref, w_ref, o_ref):
    h = _ln(x_ref[...]) * (1.0 + sc_ref[...]) + sh_ref[...]
    o_ref[...] = jnp.dot(h.astype(BF16), w_ref[...], preferred_element_type=F32).astype(o_ref.dtype)


def _proj(xf, mod4, w, seq):
    t, d = xf.shape
    n = w.shape[1]
    tpb = seq // PROJ_TM
    return pl.pallas_call(
        _proj_body,
        grid=(t // PROJ_TM,),
        in_specs=[pl.BlockSpec((PROJ_TM, d), lambda i: (i, 0)),
                  pl.BlockSpec((None, None, 1, d), lambda i: (i // tpb, 1, 0, 0)),
                  pl.BlockSpec((None, None, 1, d), lambda i: (i // tpb, 0, 0, 0)),
                  pl.BlockSpec((d, n), lambda i: (0, 0))],
        out_specs=pl.BlockSpec((PROJ_TM, n), lambda i: (i, 0)),
        out_shape=jax.ShapeDtypeStruct((t, n), BF16),
        compiler_params=_cparams("arbitrary"),
        name="proj",
    )(xf, mod4, mod4, w)


def _na_row_offsets(rows):
    kr = min(NA_WIN_ROWS, rows)
    nrb = rows // NA_QROWS
    pats = []
    for rb in range(nrb):
        r0 = rb * NA_QROWS
        s = min(max(r0 - NA_WIN_ROWS // 2, 0), rows - NA_KROWS)
        pat = []
        for qr in range(NA_QROWS):
            r = r0 + qr
            rs = min(max(r - kr // 2, 0), rows - kr)
            pat.append(tuple((s + i) - r + (NA_WIN_ROWS - 1) if rs <= s + i < rs + kr else 15
                             for i in range(NA_KROWS)))
        pats.append(tuple(pat))
    variants = [pats[0], pats[1], pats[-1]]
    assert all(p == variants[1] for p in pats[1:-1])
    return variants


def _na_bias_body(rpb_ref, o_ref, tile_ref, *, variants):
    h = pl.program_id(0)
    nro = 2 * NA_WIN_ROWS - 1
    nco = 2 * NA_WIN_COLS - 1
    q = lax.broadcasted_iota(jnp.int32, (GRID_W, LANES), 0)
    kk = lax.broadcasted_iota(jnp.int32, (GRID_W, LANES), 1)
    k = jnp.bitwise_and(kk, GRID_W - 1)
    col_start = jnp.clip(q - NA_WIN_COLS // 2, 0, GRID_W - NA_WIN_COLS)
    valid = jnp.logical_and(k >= col_start, k < col_start + NA_WIN_COLS)
    co = jnp.clip(k - q, -(NA_WIN_COLS - 1), NA_WIN_COLS - 1) + (NA_WIN_COLS - 1)
    neg = jnp.full((GRID_W, LANES), NEG_BIG, F32)
    for ro in range(nro):
        acc = neg
        for cc in range(nco):
            acc = jnp.where(co == cc, rpb_ref[(h * nro + ro) * nco + cc], acc)
        tile_ref[ro] = jnp.where(valid, acc, NEG_BIG)
    tile_ref[nro] = neg
    left = kk < GRID_W
    for v, pat in enumerate(variants):
        for qr in range(NA_QROWS):
            for kp in range(NA_KROWS // 2):
                a, b = pat[qr][2 * kp], pat[qr][2 * kp + 1]
                o_ref[v, qr * GRID_W:(qr + 1) * GRID_W, kp * LANES:(kp + 1) * LANES] = (
                    jnp.where(left, tile_ref[a], tile_ref[b]))


def _na_bias(rpb_flat, rows):
    variants = _na_row_offsets(rows)
    nq, nk = NA_QROWS * GRID_W, NA_KROWS * GRID_W
    return pl.pallas_call(
        functools.partial(_na_bias_body, variants=variants),
        grid=(NA_HEADS,),
        in_specs=[pl.BlockSpec(memory_space=pltpu.SMEM)],
        out_specs=pl.BlockSpec((3, None, nq, nk), lambda h: (0, h, 0, 0)),
        out_shape=jax.ShapeDtypeStruct((3, NA_HEADS, nq, nk), F32),
        scratch_shapes=[pltpu.VMEM((2 * NA_WIN_ROWS, GRID_W, LANES), F32)],
        compiler_params=_cparams("arbitrary"),
        name="na_bias",
    )(rpb_flat)


def _half_masks(value=1.0):
    lane = lax.broadcasted_iota(jnp.int32, (1, LANES), 1)
    lo = jnp.where(lane < LANES // 2, value, 0.0).astype(BF16)
    hi = jnp.where(lane < LANES // 2, 0.0, value).astype(BF16)
    return lo, hi


def _na_body(q_ref, k_ref, v_ref, b_ref, o_ref, *, rows):
    rb = pl.program_id(0)
    s_row = jnp.clip(rb * NA_QROWS - NA_WIN_ROWS // 2, 0, rows - NA_KROWS)
    start = pl.multiple_of(s_row * GRID_W, GRID_W)
    nk = NA_KROWS * GRID_W
    masks = list(zip(_half_masks(NA_HEAD_DIM ** -0.5), _half_masks()))
    for hp in range(NA_HEADS // 2):
        cols = slice(hp * LANES, (hp + 1) * LANES)
        q = q_ref[:, cols]
        kw = k_ref[pl.ds(start, nk), cols]
        vw = v_ref[pl.ds(start, nk), cols]
        nq = q.shape[0]
        s2 = lax.dot_general(jnp.concatenate([q * mq for mq, _ in masks], axis=0), kw, _NT,
                             preferred_element_type=F32)
        ps, inv = [], []
        for h in range(2):
            s = s2[h * nq:(h + 1) * nq] + b_ref[2 * hp + h]
            p = jnp.exp(s - jnp.max(s, axis=-1, keepdims=True))
            inv.append(1.0 / jnp.sum(p, axis=-1, keepdims=True))
            ps.append(p.astype(BF16))
        o = jnp.dot(jnp.concatenate(ps, axis=1),
                    jnp.concatenate([vw * mv for _, mv in masks], axis=0),
                    preferred_element_type=F32)
        lane = lax.broadcasted_iota(jnp.int32, (1, LANES), 1)
        o_ref[:, cols] = (o * jnp.where(lane < LANES // 2, inv[0], inv[1])).astype(o_ref.dtype)


def _na(proj3, nb, rows):
    b, seq, _ = proj3.shape
    nrb = rows // NA_QROWS
    nq, nk = NA_QROWS * GRID_W, NA_KROWS * GRID_W

    def variant(rb):
        return jnp.where(rb == 0, 0, jnp.where(rb == nrb - 1, 2, 1))

    return pl.pallas_call(
        functools.partial(_na_body, rows=rows),
        grid=(nrb, b),
        in_specs=[pl.BlockSpec((None, nq, NA_WIDTH), lambda rb, bi: (bi, rb, 0)),
                  pl.BlockSpec((None, seq, NA_WIDTH), lambda rb, bi: (bi, 0, 1)),
                  pl.BlockSpec((None, seq, NA_WIDTH), lambda rb, bi: (bi, 0, 2)),
                  pl.BlockSpec((None, NA_HEADS, nq, nk), lambda rb, bi: (variant(rb), 0, 0, 0))],
        out_specs=pl.BlockSpec((None, nq, NA_WIDTH), lambda rb, bi: (bi, rb, 0)),
        out_shape=jax.ShapeDtypeStruct((b, seq, NA_WIDTH), BF16),
        compiler_params=_cparams("arbitrary", "arbitrary"),
        name="na",
    )(proj3, proj3, proj3, nb)


def _t5_thresholds():
    nb = T5_BUCKETS // 2
    me = nb // 2
    span = nb - me
    ratio = T5_MAX_DIST // me
    out = []
    for m in range(1, span):
        n = me
        while n ** span < me ** span * ratio ** m:
            n += 1
        out.append(n)
    return out


def _t5_bias_body(t5_ref, o_ref, *, nq):
    h = pl.program_id(0)
    nb = T5_BUCKETS // 2
    me = nb // 2
    thresholds = _t5_thresholds()
    assert DIFF_TILE >= thresholds[-1]
    block = pl.program_id(1) - (nq - 1)

    @pl.when(jnp.abs(block) <= 1)
    def _():
        q = lax.broadcasted_iota(jnp.int32, (DIFF_TILE, DIFF_TILE), 0)
        k = lax.broadcasted_iota(jnp.int32, (DIFF_TILE, DIFF_TILE), 1)
        rel = block * DIFF_TILE + k - q
        n = jnp.abs(rel)
        large = jnp.full(n.shape, me, jnp.int32)
        for thr in thresholds:
            large = large + jnp.where(n >= thr, 1, 0)
        bkt = jnp.where(rel > 0, nb, 0) + jnp.where(n < me, n, large)
        acc = jnp.zeros(bkt.shape, F32)
        for bb in range(T5_BUCKETS):
            acc = jnp.where(bkt == bb, t5_ref[bb * DIFF_HEADS + h], acc)
        o_ref[...] = acc

    @pl.when(jnp.abs(block) > 1)
    def _():
        far = jnp.where(block > 0, t5_ref[(T5_BUCKETS - 1) * DIFF_HEADS + h],
                        t5_ref[(nb - 1) * DIFF_HEADS + h])
        o_ref[...] = jnp.full(o_ref.shape, far, F32)


def _t5_bias(t5_flat, seq):
    nq = seq // DIFF_TILE
    nt = 2 * nq - 1
    return pl.pallas_call(
        functools.partial(_t5_bias_body, nq=nq),
        grid=(DIFF_HEADS, nt),
        in_specs=[pl.BlockSpec(memory_space=pltpu.SMEM)],
        out_specs=pl.BlockSpec((None, None, DIFF_TILE, DIFF_TILE), lambda h, t: (h, t, 0, 0)),
        out_shape=jax.ShapeDtypeStruct((DIFF_HEADS, nt, DIFF_TILE, DIFF_TILE), F32),
        compiler_params=_cparams("arbitrary", "arbitrary"),
        name="t5_bias",
    )(t5_flat)


def _diff_body(q_ref, k_ref, v_ref, dt_ref, lq1_ref, lk1_ref, lq2_ref, lk2_ref, g_ref, o_ref,
               *, nkb, lambda_init):
    qi = pl.program_id(2)
    q = q_ref[...]
    nsub = DIFF_TQ // DIFF_TILE
    lam = (jnp.exp(jnp.sum(lq1_ref[...] * lk1_ref[...], axis=-1, keepdims=True))
           - jnp.exp(jnp.sum(lq2_ref[...] * lk2_ref[...], axis=-1, keepdims=True)) + lambda_init)
    maps = []
    for m in _half_masks(DIFF_HEAD_DIM ** -0.5):
        qm = q * m
        pieces = []
        for kb in range(nkb):
            s = lax.dot_general(qm, k_ref[kb * DIFF_TILE:(kb + 1) * DIFF_TILE, :], _NT,
                                preferred_element_type=F32)
            bias = [dt_ref[nkb - 1 - (nsub * qi + hq) + kb] for hq in range(nsub)]
            pieces.append(s + jnp.concatenate(bias, axis=0))
        s = jnp.concatenate(pieces, axis=1)
        p = jnp.exp(s - jnp.max(s, axis=-1, keepdims=True))
        maps.append((p, jnp.sum(p, axis=-1, keepdims=True)))
    (p1, l1), (p2, l2) = maps
    w = p1 * (1.0 / l1) - p2 * (lam / l2)
    o = jnp.dot(w.astype(BF16), v_ref[...], preferred_element_type=F32)
    o = o * lax.rsqrt(jnp.mean(o * o, axis=-1, keepdims=True) + LN_EPS) * g_ref[...]
    o_ref[...] = (o * (1.0 - lambda_init)).astype(o_ref.dtype)


def _diff(proj3, dt, lq1, lk1, lq2, lk2, subln, lambda_init):
    b, seq, _ = proj3.shape
    nkb = seq // DIFF_TILE
    nt = 2 * nkb - 1
    c0 = 3 * NA_WIDTH // LANES
    vec = lambda n: pl.BlockSpec((1, n), lambda h, bi, qi: (0, 0))
    return pl.pallas_call(
        functools.partial(_diff_body, nkb=nkb, lambda_init=lambda_init),
        grid=(DIFF_HEADS, b, seq // DIFF_TQ),
        in_specs=[pl.BlockSpec((None, DIFF_TQ, LANES), lambda h, bi, qi: (bi, qi, c0 + h)),
                  pl.BlockSpec((None, seq, LANES), lambda h, bi, qi: (bi, 0, c0 + DIFF_HEADS + h)),
                  pl.BlockSpec((None, seq, LANES), lambda h, bi, qi: (bi, 0, c0 + 2 * DIFF_HEADS + h)),
                  pl.BlockSpec((None, nt, DIFF_TILE, DIFF_TILE), lambda h, bi, qi: (h, 0, 0, 0)),
                  vec(DIFF_HEAD_DIM), vec(DIFF_HEAD_DIM), vec(DIFF_HEAD_DIM), vec(DIFF_HEAD_DIM),
                  vec(DIFF_V_DIM)],
        out_specs=pl.BlockSpec((None, DIFF_TQ, LANES), lambda h, bi, qi: (bi, qi, h)),
        out_shape=jax.ShapeDtypeStruct((b, seq, DIFF_WIDTH), BF16),
        compiler_params=_cparams("arbitrary", "arbitrary", "arbitrary"),
        name="diff",
    )(proj3, proj3, proj3, dt, lq1, lk1, lq2, lk2, subln)


def _mid_body(ona_ref, odf_ref, wout_ref, x_ref, ga_ref, scf_ref, shf_ref, g1_ref, b1_ref,
              wq_ref, sk_ref, x1_ref, h2_ref, st_ref):
    mix = (jnp.dot(ona_ref[...], wout_ref[0:NA_WIDTH, :], preferred_element_type=F32)
           + jnp.dot(odf_ref[...], wout_ref[NA_WIDTH:, :], preferred_element_type=F32))
    y = DN_ALPHA * x_ref[...] + (1.0 + ga_ref[...]) * mix
    x1 = _ln(y) * g1_ref[...] + b1_ref[...]
    x1_ref[...] = x1
    h2 = (_ln(x1) * (1.0 + scf_ref[...]) + shf_ref[...]).astype(BF16)
    h2_ref[...] = h2
    qp = jnp.dot(h2, wq_ref[...], preferred_element_type=F32).astype(BF16)
    for hp in range(2 * PEER_HEADS):
        st_ref[hp] = lax.dot_general(sk_ref[hp % 2], qp[:, hp * PEER_NKEYS:(hp + 1) * PEER_NKEYS],
                                     _NT, preferred_element_type=F32)


def _mid(o_na, o_df, w_out, xf, mod4, ln_g, ln_b, w_query, sub_keys, seq):
    t, d = xf.shape
    tpb = seq // MID_TM
    nqk = w_query.shape[1]
    modspec = lambda j: pl.BlockSpec((None, None, 1, d), lambda i: (i // tpb, j, 0, 0))
    full = lambda *s: pl.BlockSpec(s, lambda i: (0,) * len(s))
    return pl.pallas_call(
        _mid_body,
        grid=(t // MID_TM,),
        in_specs=[pl.BlockSpec((MID_TM, NA_WIDTH), lambda i: (i, 0)),
                  pl.BlockSpec((MID_TM, DIFF_WIDTH), lambda i: (i, 0)),
                  full(d, d),
                  pl.BlockSpec((MID_TM, d), lambda i: (i, 0)),
                  modspec(2), modspec(4), modspec(3),
                  full(1, d), full(1, d),
                  full(d, nqk),
                  full(2, PEER_NKEYS, PEER_NKEYS)],
        out_specs=[pl.BlockSpec((MID_TM, d), lambda i: (i, 0)),
                   pl.BlockSpec((MID_TM, d), lambda i: (i, 0)),
                   pl.BlockSpec((2 * PEER_HEADS, PEER_NKEYS, MID_TM), lambda i: (0, 0, i))],
        out_shape=[jax.ShapeDtypeStruct((t, d), F32),
                   jax.ShapeDtypeStruct((t, d), BF16),
                   jax.ShapeDtypeStruct((2 * PEER_HEADS, PEER_NKEYS, t), F32)],
        compiler_params=_cparams("arbitrary"),
        name="mid",
    )(o_na, o_df, w_out, xf, mod4, mod4, mod4, ln_g, ln_b, w_query, sub_keys)


def _extract_top(work, order, tie_rule, out_ref=None, tok=None):
    rank = jnp.full(work.shape, float(PEER_TOPK), F32)
    big = float(PEER_N)
    for r in range(PEER_TOPK):
        m = jnp.max(work, axis=0, keepdims=True)
        sel = work == m
        if tie_rule:
            first = jnp.min(jnp.where(sel, order, big), axis=0, keepdims=True)
            sel = order == first
        rank = jnp.where(sel, float(r), rank)
        work = jnp.where(sel, -jnp.inf, work)
        if out_ref is not None:
            out_ref[r:r + 1, tok] = m
    return rank


def _column_max(x):
    slabs = [x[i:i + 8] for i in range(0, x.shape[0], 8)]
    while len(slabs) > 1:
        slabs = [jnp.maximum(slabs[i], slabs[i + 1]) if i + 1 < len(slabs) else slabs[i]
                 for i in range(0, len(slabs), 2)]
    return jnp.max(slabs[0], axis=0, keepdims=True)


def _pair_cells(op, u1, u2):
    g = [op(u1[0:1], u2[0:8]), op(u1[0:1], u2[8:16])]
    for a in range(1, 8):
        g.append(op(u1[a:a + 1], u2[0:8]))
    g.append(op(u1[8:16], u2[0:1]))
    return jnp.concatenate(g, axis=0)


def _topk_body(st_ref, lc_ref, f1_ref, r2_ref, f2_ref, t_ref, rk_ref, l_ref, z_ref, w2_ref):
    tl = st_ref.shape[-1]
    chunks = [slice(c * LANES, (c + 1) * LANES) for c in range(tl // LANES)]
    key_id = lax.broadcasted_iota(jnp.int32, (PEER_NKEYS, LANES), 0).astype(F32)

    def half_ranks(tie_rule):
        for tok in chunks:
            for p in range(2):
                rk_ref[p, :, tok] = _extract_top(st_ref[p, :, tok], key_id, tie_rule,
                                                 t_ref.at[p], tok)

    jobs = [(p, tok) for tok in chunks for p in range(2)]
    for j0 in range(0, len(jobs), FAST_CHAINS):
        group = jobs[j0:j0 + FAST_CHAINS]
        work = [st_ref[p, :, tok] for p, tok in group]
        for p, tok in group:
            rk_ref[p, :, tok] = jnp.full((PEER_NKEYS, LANES), float(PEER_TOPK), F32)
        for r in range(PEER_TOPK):
            for i, (p, tok) in enumerate(group):
                m = _column_max(work[i])
                sel = work[i] == m
                rk_ref[p, :, tok] = jnp.where(sel, float(r), rk_ref[p, :, tok])
                work[i] = jnp.where(sel, -jnp.inf, work[i])
                t_ref[p, r:r + 1, tok] = m
    taken = jnp.where(rk_ref[...] < float(PEER_TOPK), 1.0, 0.0)
    most = jnp.max(jnp.sum(taken, axis=1, keepdims=True))

    @pl.when(most > float(PEER_TOPK))
    def _():
        half_ranks(True)

    wide = 4 * LANES
    row = lax.broadcasted_iota(jnp.int32, (80, wide), 0)
    mid = row - 16
    cell_id = jnp.where(row < 16, row,
                        jnp.where(row < 72,
                                  (jnp.right_shift(mid, 3) + 1) * 16 + jnp.bitwise_and(mid, 7),
                                  (row - 64) * 16)).astype(F32)
    for c0 in range(0, tl, wide):
        tok2 = slice(c0, c0 + wide)
        t1 = t_ref[0, :, tok2]
        t2 = t_ref[1, :, tok2]
        work = _pair_cells(jnp.add, t1, t2)
        for _ in range(PEER_TOPK):
            work = jnp.where(work == _column_max(work), -jnp.inf, work)
        w2_ref[:, tok2] = work
        taken = jnp.sum(jnp.where(work == -jnp.inf, 1.0, 0.0), axis=0, keepdims=True)

        @pl.when(jnp.max(taken) > float(PEER_TOPK))
        def _():
            again = _pair_cells(jnp.add, t_ref[0, :, tok2], t_ref[1, :, tok2])
            for _ in range(PEER_TOPK):
                m = jnp.max(again, axis=0, keepdims=True)
                first = jnp.min(jnp.where(again == m, cell_id, float(PEER_N)), axis=0,
                                keepdims=True)
                again = jnp.where(cell_id == first, -jnp.inf, again)
            w2_ref[:, tok2] = again

        chosen = jnp.where(w2_ref[:, tok2] == -jnp.inf, 1.0, 0.0)
        e1 = jnp.exp(t1 - t1[0:1])
        e2 = jnp.exp(t2 - t2[0:1])
        z = jnp.sum(chosen * _pair_cells(jnp.multiply, e1, e2), axis=0, keepdims=True)
        z_ref[0:1, tok2] = 1.0 / z
        l_ref[0:1, tok2] = jnp.sum(chosen[0:16], axis=0, keepdims=True)
        for a in range(1, 8):
            l_ref[a:a + 1, tok2] = jnp.sum(chosen[8 + 8 * a:16 + 8 * a], axis=0, keepdims=True)
        l_ref[8:16, tok2] = chosen[72:80]
    for tok in chunks:
        rank1 = rk_ref[0, :, tok]
        rank2 = rk_ref[1, :, tok]
        lcnt = l_ref[:, tok]
        lc = jnp.zeros((PEER_NKEYS, LANES), F32)
        for a in range(PEER_TOPK):
            lc = jnp.where(rank1 == float(a), lcnt[a:a + 1], lc)
        lc_ref[:, tok] = lc
        f1_ref[:, tok] = jnp.exp(st_ref[0, :, tok] - t_ref[0, 0:1, tok]) * z_ref[0:1, tok]
        r2_ref[:, tok] = rank2
        f2_ref[:, tok] = jnp.where(rank2 < float(PEER_TOPK),
                                   jnp.exp(st_ref[1, :, tok] - t_ref[1, 0:1, tok]), 0.0)


def _topk(st):
    _, nk, t = st.shape
    tl = min(TOPK_TL, t)
    spec = pl.BlockSpec((None, nk, tl), lambda i, h: (h, 0, i))
    shp = jax.ShapeDtypeStruct((PEER_HEADS, nk, t), F32)
    return pl.pallas_call(
        _topk_body,
        grid=(t // tl, PEER_HEADS),
        in_specs=[pl.BlockSpec((2, nk, tl), lambda i, h: (h, 0, i))],
        out_specs=[spec, spec, spec, spec],
        out_shape=[shp, shp, shp, shp],
        scratch_shapes=[pltpu.VMEM((2, PEER_TOPK, tl), F32), pltpu.VMEM((2, nk, tl), F32),
                        pltpu.VMEM((PEER_TOPK, tl), F32), pltpu.VMEM((8, tl), F32),
                        pltpu.VMEM((80, tl), F32)],
        compiler_params=_cparams("arbitrary", "arbitrary"),
        name="topk",
    )(st)


def _peer_body(h2_ref, ed_ref, eut_ref, lc_ref, f1_ref, r2_ref, f2_ref, x1_ref, gf_ref,
               g2_ref, b2_ref, o_ref, acc_ref, *, ni):
    et = pl.program_id(1)

    @pl.when(et == 0)
    def _():
        acc_ref[...] = jnp.zeros(acc_ref.shape, F32)

    tm = h2_ref.shape[0]
    sub = BF16_SUBLANES
    kgroup = 2 * sub
    at = lax.dot_general(ed_ref[...], h2_ref[...], _NT, preferred_element_type=F32)
    cols = []
    for c in range(tm // LANES):
        tok = slice(c * LANES, (c + 1) * LANES)
        w_rows = {}
        for k0 in range(0, PEER_NKEYS, kgroup):
            subs = [slice(k0 + j * sub, k0 + (j + 1) * sub) for j in range(kgroup // sub)]
            rank = [[r2_ref[h, rk, tok].astype(BF16) for rk in subs] for h in range(PEER_HEADS)]
            val = [[f2_ref[h, rk, tok].astype(BF16) for rk in subs] for h in range(PEER_HEADS)]
            for ii in range(ni):
                g = [None] * len(subs)
                for h in range(PEER_HEADS):
                    cnt = jnp.broadcast_to(lc_ref[h, ii:ii + 1, tok], (sub, LANES)).astype(BF16)
                    fac = jnp.broadcast_to(f1_ref[h, ii:ii + 1, tok], (sub, LANES)).astype(BF16)
                    for j in range(len(subs)):
                        term = jnp.where(rank[h][j] < cnt, val[h][j], 0) * fac
                        g[j] = term if g[j] is None else g[j] + term
                for j, rk in enumerate(subs):
                    r0 = ii * PEER_NKEYS + rk.start
                    a = at[r0:r0 + sub, tok]
                    act = 0.5 * a * (1.0 + lax.erf(a * (2.0 ** -0.5)))
                    w_rows[r0] = g[j] * act.astype(BF16)
        cols.append(jnp.concatenate([w_rows[r0] for r0 in sorted(w_rows)], axis=0))
    wt = jnp.concatenate(cols, axis=1)
    acc_ref[...] += jnp.dot(eut_ref[...], wt, preferred_element_type=F32)

    @pl.when(et == pl.num_programs(1) - 1)
    def _():
        f = acc_ref[...].T
        y = DN_ALPHA * x1_ref[...] + (1.0 + gf_ref[...]) * f
        o_ref[...] = _ln(y) * g2_ref[...] + b2_ref[...]


def _peer(h2, ed, eut, lc, f1, r2, f2, x1, mod4, ln_g, ln_b, seq):
    t, d = h2.shape
    tm = min(PEER_TM, t)
    te = PEER_TE
    ni = te // PEER_NKEYS
    ne = PEER_N // te
    tpb = seq // tm
    lc4 = lc.reshape(PEER_HEADS, ne, ni, t)
    f14 = f1.reshape(PEER_HEADS, ne, ni, t)
    rowspec = pl.BlockSpec((PEER_HEADS, None, ni, tm), lambda i, e: (0, e, 0, i))
    colspec = pl.BlockSpec((PEER_HEADS, PEER_NKEYS, tm), lambda i, e: (0, 0, i))
    full = lambda *s: pl.BlockSpec(s, lambda i, e: (0,) * len(s))
    return pl.pallas_call(
        functools.partial(_peer_body, ni=ni),
        grid=(t // tm, ne),
        in_specs=[pl.BlockSpec((tm, d), lambda i, e: (i, 0)),
                  pl.BlockSpec((te, d), lambda i, e: (e, 0)),
                  pl.BlockSpec((d, te), lambda i, e: (0, e)),
                  rowspec, rowspec, colspec, colspec,
                  pl.BlockSpec((tm, d), lambda i, e: (i, 0)),
                  pl.BlockSpec((None, None, 1, d), lambda i, e: (i // tpb, 5, 0, 0)),
                  full(1, d), full(1, d)],
        out_specs=pl.BlockSpec((tm, d), lambda i, e: (i, 0)),
        out_shape=jax.ShapeDtypeStruct((t, d), F32),
        scratch_shapes=[pltpu.VMEM((d, tm), F32)],
        compiler_params=_cparams("arbitrary", "arbitrary"),
        name="peer",
    )(h2, ed, eut, lc4, f14, r2, f2, x1, mod4, ln_g, ln_b)


def kernel(x, c, w_ada, b_ada, w_in, w_out, na_rpb, t5_bias, lambda_q1, lambda_k1, lambda_q2,
           lambda_k2, diff_subln, ln1_g, ln1_b, w_query, sub_keys, expert_down, expert_up,
           ln2_g, ln2_b):
    b, seq, d = x.shape
    t = b * seq
    rows = seq // GRID_W
    assert d == D_MODEL and w_ada.shape[0] == DEPTH == 1
    assert rows % NA_QROWS == 0 and rows >= NA_KROWS and seq % DIFF_TQ == 0
    l = 0
    lambda_init = 0.8 - 0.6 * math.exp(-0.3 * l)
    xf = x.reshape(t, d)

    mod4 = _ada(c, w_ada[l], b_ada[l][None, :]).reshape(b, 6, 1, d)
    proj3 = _proj(xf, mod4, w_in[l].astype(BF16), seq).reshape(b, seq, IN_WIDTH)

    nb = _na_bias(na_rpb[l].reshape(-1), rows)
    o_na = _na(proj3, nb, rows).reshape(t, NA_WIDTH)

    dt = _t5_bias(t5_bias.reshape(-1), seq)
    o_df = _diff(proj3, dt, lambda_q1[l][None, :], lambda_k1[l][None, :], lambda_q2[l][None, :],
                 lambda_k2[l][None, :], diff_subln[l][None, :], lambda_init).reshape(t, DIFF_WIDTH)

    x1, h2, st = _mid(o_na, o_df, w_out[l].astype(BF16), xf, mod4, ln1_g[l][None, :],
                      ln1_b[l][None, :], w_query[l].astype(BF16), sub_keys[l].astype(BF16), seq)
    lc, f1, r2, f2 = _topk(st)
    out = _peer(h2, expert_down[l].astype(BF16), expert_up[l].astype(BF16).T, lc, f1, r2, f2,
                x1, mod4, ln2_g[l][None, :], ln2_b[l][None, :], seq)
    return out.reshape(b, seq, d)
```

```python
import functools
import math

import jax
import jax.numpy as jnp
from jax import lax
from jax.experimental import pallas as pl
from jax.experimental.pallas import tpu as pltpu

F32 = jnp.float32
BF16 = jnp.bfloat16

D_MODEL = 1024
DEPTH = 1
GRID_W = 64
NA_HEADS = 8
NA_HEAD_DIM = 64
NA_WIDTH = NA_HEADS * NA_HEAD_DIM
NA_WIN_ROWS = 8
NA_WIN_COLS = 16
DIFF_HEADS = 4
DIFF_HEAD_DIM = 64
DIFF_V_DIM = 2 * DIFF_HEAD_DIM
DIFF_WIDTH = DIFF_HEADS * DIFF_V_DIM
IN_WIDTH = 3 * NA_WIDTH + 3 * DIFF_WIDTH
T5_BUCKETS = 32
T5_MAX_DIST = 128
PEER_HEADS = 8
PEER_NKEYS = 128
PEER_N = PEER_NKEYS * PEER_NKEYS
PEER_TOPK = 16
LN_EPS = 1e-5
DN_ALPHA = (2.0 * DEPTH) ** 0.25
NEG_BIG = -1e30
LANES = 128
BF16_SUBLANES = 16

PROJ_TM = 512
NA_QROWS = 4
NA_KROWS = 12
DIFF_TQ = 256
DIFF_TILE = 256
MID_TM = 512
TOPK_TL = 1024
FAST_CHAINS = 4
PEER_TM = 512
PEER_TE = 2048
VMEM_LIMIT = 56 * 1024 * 1024

_NT = (((1,), (1,)), ((), ()))


def _cparams(*sem):
    return pltpu.CompilerParams(dimension_semantics=sem, vmem_limit_bytes=VMEM_LIMIT)


def _ln(x):
    mu = jnp.mean(x, axis=-1, keepdims=True)
    xc = x - mu
    var = jnp.mean(xc * xc, axis=-1, keepdims=True)
    return xc * lax.rsqrt(var + LN_EPS)


def _ada_body(c_ref, w_ref, b_ref, o_ref):
    c = c_ref[...]
    s = c * jax.nn.sigmoid(c)
    o_ref[...] = jnp.dot(s, w_ref[...], preferred_element_type=F32) + b_ref[...]


def _ada(c, w, bias):
    b, d = c.shape
    n = w.shape[1]
    return pl.pallas_call(
        _ada_body,
        grid=(n // d,),
        in_specs=[pl.BlockSpec((b, d), lambda j: (0, 0)),
                  pl.BlockSpec((d, d), lambda j: (0, j)),
                  pl.BlockSpec((1, d), lambda j: (0, j))],
        out_specs=pl.BlockSpec((b, d), lambda j: (0, j)),
        out_shape=jax.ShapeDtypeStruct((b, n), F32),
        compiler_params=_cparams("arbitrary"),
        name="ada",
    )(c, w, bias)


def _proj_body(x_ref, sc_ref, sh_ref, w_ref, o_ref):
    h = _ln(x_ref[...]) * (1.0 + sc_ref[...]) + sh_ref[...]
    o_ref[...] = jnp.dot(h.astype(BF16), w_ref[...], preferred_element_type=F32).astype(o_ref.dtype)


def _proj(xf, mod4, w, seq):
    t, d = xf.shape
    n = w.shape[1]
    tpb = seq // PROJ_TM
    return pl.pallas_call(
        _proj_body,
        grid=(t // PROJ_TM,),
        in_specs=[pl.BlockSpec((PROJ_TM, d), lambda i: (i, 0)),
                  pl.BlockSpec((None, None, 1, d), lambda i: (i // tpb, 1, 0, 0)),
                  pl.BlockSpec((None, None, 1, d), lambda i: (i // tpb, 0, 0, 0)),
                  pl.BlockSpec((d, n), lambda i: (0, 0))],
        out_specs=pl.BlockSpec((PROJ_TM, n), lambda i: (i, 0)),
        out_shape=jax.ShapeDtypeStruct((t, n), BF16),
        compiler_params=_cparams("arbitrary"),
        name="proj",
    )(xf, mod4, mod4, w)


def _na_row_offsets(rows):
    kr = min(NA_WIN_ROWS, rows)
    nrb = rows // NA_QROWS
    pats = []
    for rb in range(nrb):
        r0 = rb * NA_QROWS
        s = min(max(r0 - NA_WIN_ROWS // 2, 0), rows - NA_KROWS)
        pat = []
        for qr in range(NA_QROWS):
            r = r0 + qr
            rs = min(max(r - kr // 2, 0), rows - kr)
            pat.append(tuple((s + i) - r + (NA_WIN_ROWS - 1) if rs <= s + i < rs + kr else 15
                             for i in range(NA_KROWS)))
        pats.append(tuple(pat))
    variants = [pats[0], pats[1], pats[-1]]
    assert all(p == variants[1] for p in pats[1:-1])
    return variants


def _na_bias_body(rpb_ref, o_ref, tile_ref, *, variants):
    h = pl.program_id(0)
    nro = 2 * NA_WIN_ROWS - 1
    nco = 2 * NA_WIN_COLS - 1
    q = lax.broadcasted_iota(jnp.int32, (GRID_W, LANES), 0)
    kk = lax.broadcasted_iota(jnp.int32, (GRID_W, LANES), 1)
    k = jnp.bitwise_and(kk, GRID_W - 1)
    col_start = jnp.clip(q - NA_WIN_COLS // 2, 0, GRID_W - NA_WIN_COLS)
    valid = jnp.logical_and(k >= col_start, k < col_start + NA_WIN_COLS)
    co = jnp.clip(k - q, -(NA_WIN_COLS - 1), NA_WIN_COLS - 1) + (NA_WIN_COLS - 1)
    neg = jnp.full((GRID_W, LANES), NEG_BIG, F32)
    for ro in range(nro):
        acc = neg
        for cc in range(nco):
            acc = jnp.where(co == cc, rpb_ref[(h * nro + ro) * nco + cc], acc)
        tile_ref[ro] = jnp.where(valid, acc, NEG_BIG)
    tile_ref[nro] = neg
    left = kk < GRID_W
    for v, pat in enumerate(variants):
        for qr in range(NA_QROWS):
            for kp in range(NA_KROWS // 2):
                a, b = pat[qr][2 * kp], pat[qr][2 * kp + 1]
                o_ref[v, qr * GRID_W:(qr + 1) * GRID_W, kp * LANES:(kp + 1) * LANES] = (
                    jnp.where(left, tile_ref[a], tile_ref[b]))


def _na_bias(rpb_flat, rows):
    variants = _na_row_offsets(rows)
    nq, nk = NA_QROWS * GRID_W, NA_KROWS * GRID_W
    return pl.pallas_call(
        functools.partial(_na_bias_body, variants=variants),
        grid=(NA_HEADS,),
        in_specs=[pl.BlockSpec(memory_space=pltpu.SMEM)],
        out_specs=pl.BlockSpec((3, None, nq, nk), lambda h: (0, h, 0, 0)),
        out_shape=jax.ShapeDtypeStruct((3, NA_HEADS, nq, nk), F32),
        scratch_shapes=[pltpu.VMEM((2 * NA_WIN_ROWS, GRID_W, LANES), F32)],
        compiler_params=_cparams("arbitrary"),
        name="na_bias",
    )(rpb_flat)


def _half_masks(value=1.0):
    lane = lax.broadcasted_iota(jnp.int32, (1, LANES), 1)
    lo = jnp.where(lane < LANES // 2, value, 0.0).astype(BF16)
    hi = jnp.where(lane < LANES // 2, 0.0, value).astype(BF16)
    return lo, hi


def _na_body(q_ref, k_ref, v_ref, b_ref, o_ref, *, rows):
    rb = pl.program_id(0)
    s_row = jnp.clip(rb * NA_QROWS - NA_WIN_ROWS // 2, 0, rows - NA_KROWS)
    start = pl.multiple_of(s_row * GRID_W, GRID_W)
    nk = NA_KROWS * GRID_W
    masks = list(zip(_half_masks(NA_HEAD_DIM ** -0.5), _half_masks()))
    for hp in range(NA_HEADS // 2):
        cols = slice(hp * LANES, (hp + 1) * LANES)
        q = q_ref[:, cols]
        kw = k_ref[pl.ds(start, nk), cols]
        vw = v_ref[pl.ds(start, nk), cols]
        nq = q.shape[0]
        s2 = lax.dot_general(jnp.concatenate([q * mq for mq, _ in masks], axis=0), kw, _NT,
                             preferred_element_type=F32)
        ps, inv = [], []
        for h in range(2):
            s = s2[h * nq:(h + 1) * nq] + b_ref[2 * hp + h]
            p = jnp.exp(s - jnp.max(s, axis=-1, keepdims=True))
            inv.append(1.0 / jnp.sum(p, axis=-1, keepdims=True))
            ps.append(p.astype(BF16))
        o = jnp.dot(jnp.concatenate(ps, axis=1),
                    jnp.concatenate([vw * mv for _, mv in masks], axis=0),
                    preferred_element_type=F32)
        lane = lax.broadcasted_iota(jnp.int32, (1, LANES), 1)
        o_ref[:, cols] = (o * jnp.where(lane < LANES // 2, inv[0], inv[1])).astype(o_ref.dtype)


def _na(proj3, nb, rows):
    b, seq, _ = proj3.shape
    nrb = rows // NA_QROWS
    nq, nk = NA_QROWS * GRID_W, NA_KROWS * GRID_W

    def variant(rb):
        return jnp.where(rb == 0, 0, jnp.where(rb == nrb - 1, 2, 1))

    return pl.pallas_call(
        functools.partial(_na_body, rows=rows),
        grid=(nrb, b),
        in_specs=[pl.BlockSpec((None, nq, NA_WIDTH), lambda rb, bi: (bi, rb, 0)),
                  pl.BlockSpec((None, seq, NA_WIDTH), lambda rb, bi: (bi, 0, 1)),
                  pl.BlockSpec((None, seq, NA_WIDTH), lambda rb, bi: (bi, 0, 2)),
                  pl.BlockSpec((None, NA_HEADS, nq, nk), lambda rb, bi: (variant(rb), 0, 0, 0))],
        out_specs=pl.BlockSpec((None, nq, NA_WIDTH), lambda rb, bi: (bi, rb, 0)),
        out_shape=jax.ShapeDtypeStruct((b, seq, NA_WIDTH), BF16),
        compiler_params=_cparams("arbitrary", "arbitrary"),
        name="na",
    )(proj3, proj3, proj3, nb)


def _t5_thresholds():
    nb = T5_BUCKETS // 2
    me = nb // 2
    span = nb - me
    ratio = T5_MAX_DIST // me
    out = []
    for m in range(1, span):
        n = me
        while n ** span < me ** span * ratio ** m:
            n += 1
        out.append(n)
    return out


def _t5_bias_body(t5_ref, o_ref, *, nq):
    h = pl.program_id(0)
    nb = T5_BUCKETS // 2
    me = nb // 2
    thresholds = _t5_thresholds()
    assert DIFF_TILE >= thresholds[-1]
    block = pl.program_id(1) - (nq - 1)

    @pl.when(jnp.abs(block) <= 1)
    def _():
        q = lax.broadcasted_iota(jnp.int32, (DIFF_TILE, DIFF_TILE), 0)
        k = lax.broadcasted_iota(jnp.int32, (DIFF_TILE, DIFF_TILE), 1)
        rel = block * DIFF_TILE + k - q
        n = jnp.abs(rel)
        large = jnp.full(n.shape, me, jnp.int32)
        for thr in thresholds:
            large = large + jnp.where(n >= thr, 1, 0)
        bkt = jnp.where(rel > 0, nb, 0) + jnp.where(n < me, n, large)
        acc = jnp.zeros(bkt.shape, F32)
        for bb in range(T5_BUCKETS):
            acc = jnp.where(bkt == bb, t5_ref[bb * DIFF_HEADS + h], acc)
        o_ref[...] = acc

    @pl.when(jnp.abs(block) > 1)
    def _():
        far = jnp.where(block > 0, t5_ref[(T5_BUCKETS - 1) * DIFF_HEADS + h],
                        t5_ref[(nb - 1) * DIFF_HEADS + h])
        o_ref[...] = jnp.full(o_ref.shape, far, F32)


def _t5_bias(t5_flat, seq):
    nq = seq // DIFF_TILE
    nt = 2 * nq - 1
    return pl.pallas_call(
        functools.partial(_t5_bias_body, nq=nq),
        grid=(DIFF_HEADS, nt),
        in_specs=[pl.BlockSpec(memory_space=pltpu.SMEM)],
        out_specs=pl.BlockSpec((None, None, DIFF_TILE, DIFF_TILE), lambda h, t: (h, t, 0, 0)),
        out_shape=jax.ShapeDtypeStruct((DIFF_HEADS, nt, DIFF_TILE, DIFF_TILE), F32),
        compiler_params=_cparams("arbitrary", "arbitrary"),
        name="t5_bias",
    )(t5_flat)


def _diff_body(q_ref, k_ref, v_ref, dt_ref, lq1_ref, lk1_ref, lq2_ref, lk2_ref, g_ref, o_ref,
               *, nkb, lambda_init):
    qi = pl.program_id(2)
    q = q_ref[...]
    nsub = DIFF_TQ // DIFF_TILE
    lam = (jnp.exp(jnp.sum(lq1_ref[...] * lk1_ref[...], axis=-1, keepdims=True))
           - jnp.exp(jnp.sum(lq2_ref[...] * lk2_ref[...], axis=-1, keepdims=True)) + lambda_init)
    maps = []
    for m in _half_masks(DIFF_HEAD_DIM ** -0.5):
        qm = q * m
        pieces = []
        for kb in range(nkb):
            s = lax.dot_general(qm, k_ref[kb * DIFF_TILE:(kb + 1) * DIFF_TILE, :], _NT,
                                preferred_element_type=F32)
            bias = [dt_ref[nkb - 1 - (nsub * qi + hq) + kb] for hq in range(nsub)]
            pieces.append(s + jnp.concatenate(bias, axis=0))
        s = jnp.concatenate(pieces, axis=1)
        p = jnp.exp(s - jnp.max(s, axis=-1, keepdims=True))
        maps.append((p, jnp.sum(p, axis=-1, keepdims=True)))
    (p1, l1), (p2, l2) = maps
    w = p1 * (1.0 / l1) - p2 * (lam / l2)
    o = jnp.dot(w.astype(BF16), v_ref[...], preferred_element_type=F32)
    o = o * lax.rsqrt(jnp.mean(o * o, axis=-1, keepdims=True) + LN_EPS) * g_ref[...]
    o_ref[...] = (o * (1.0 - lambda_init)).astype(o_ref.dtype)


def _diff(proj3, dt, lq1, lk1, lq2, lk2, subln, lambda_init):
    b, seq, _ = proj3.shape
    nkb = seq // DIFF_TILE
    nt = 2 * nkb - 1
    c0 = 3 * NA_WIDTH // LANES
    vec = lambda n: pl.BlockSpec((1, n), lambda h, bi, qi: (0, 0))
    return pl.pallas_call(
        functools.partial(_diff_body, nkb=nkb, lambda_init=lambda_init),
        grid=(DIFF_HEADS, b, seq // DIFF_TQ),
        in_specs=[pl.BlockSpec((None, DIFF_TQ, LANES), lambda h, bi, qi: (bi, qi, c0 + h)),
                  pl.BlockSpec((None, seq, LANES), lambda h, bi, qi: (bi, 0, c0 + DIFF_HEADS + h)),
                  pl.BlockSpec((None, seq, LANES), lambda h, bi, qi: (bi, 0, c0 + 2 * DIFF_HEADS + h)),
                  pl.BlockSpec((None, nt, DIFF_TILE, DIFF_TILE), lambda h, bi, qi: (h, 0, 0, 0)),
                  vec(DIFF_HEAD_DIM), vec(DIFF_HEAD_DIM), vec(DIFF_HEAD_DIM), vec(DIFF_HEAD_DIM),
                  vec(DIFF_V_DIM)],
        out_specs=pl.BlockSpec((None, DIFF_TQ, LANES), lambda h, bi, qi: (bi, qi, h)),
        out_shape=jax.ShapeDtypeStruct((b, seq, DIFF_WIDTH), BF16),
        compiler_params=_cparams("arbitrary", "arbitrary", "arbitrary"),
        name="diff",
    )(proj3, proj3, proj3, dt, lq1, lk1, lq2, lk2, subln)


def _mid_body(ona_ref, odf_ref, wout_ref, x_ref, ga_ref, scf_ref, shf_ref, g1_ref, b1_ref,
              wq_ref, sk_ref, x1_ref, h2_ref, st_ref):
    mix = jnp.dot(jnp.concatenate([ona_ref[...], odf_ref[...]], axis=1), wout_ref[...],
                  preferred_element_type=F32)
    y = DN_ALPHA * x_ref[...] + (1.0 + ga_ref[...]) * mix
    x1 = _ln(y) * g1_ref[...] + b1_ref[...]
    x1_ref[...] = x1
    h2 = (_ln(x1) * (1.0 + scf_ref[...]) + shf_ref[...]).astype(BF16)
    h2_ref[...] = h2
    qp = jnp.dot(h2, wq_ref[...], preferred_element_type=F32).astype(BF16)
    for hp in range(2 * PEER_HEADS):
        st_ref[hp] = lax.dot_general(sk_ref[hp % 2], qp[:, hp * PEER_NKEYS:(hp + 1) * PEER_NKEYS],
                                     _NT, preferred_element_type=F32)


def _mid(o_na, o_df, w_out, xf, mod4, ln_g, ln_b, w_query, sub_keys, seq):
    t, d = xf.shape
    tpb = seq // MID_TM
    nqk = w_query.shape[1]
    modspec = lambda j: pl.BlockSpec((None, None, 1, d), lambda i: (i // tpb, j, 0, 0))
    full = lambda *s: pl.BlockSpec(s, lambda i: (0,) * len(s))
    return pl.pallas_call(
        _mid_body,
        grid=(t // MID_TM,),
        in_specs=[pl.BlockSpec((MID_TM, NA_WIDTH), lambda i: (i, 0)),
                  pl.BlockSpec((MID_TM, DIFF_WIDTH), lambda i: (i, 0)),
                  full(d, d),
                  pl.BlockSpec((MID_TM, d), lambda i: (i, 0)),
                  modspec(2), modspec(4), modspec(3),
                  full(1, d), full(1, d),
                  full(d, nqk),
                  full(2, PEER_NKEYS, PEER_NKEYS)],
        out_specs=[pl.BlockSpec((MID_TM, d), lambda i: (i, 0)),
                   pl.BlockSpec((MID_TM, d), lambda i: (i, 0)),
                   pl.BlockSpec((2 * PEER_HEADS, PEER_NKEYS, MID_TM), lambda i: (0, 0, i))],
        out_shape=[jax.ShapeDtypeStruct((t, d), F32),
                   jax.ShapeDtypeStruct((t, d), BF16),
                   jax.ShapeDtypeStruct((2 * PEER_HEADS, PEER_NKEYS, t), F32)],
        compiler_params=_cparams("arbitrary"),
        name="mid",
    )(o_na, o_df, w_out, xf, mod4, mod4, mod4, ln_g, ln_b, w_query, sub_keys)


def _extract_top(work, order, tie_rule, out_ref=None, tok=None):
    rank = jnp.full(work.shape, float(PEER_TOPK), F32)
    big = float(PEER_N)
    for r in range(PEER_TOPK):
        m = jnp.max(work, axis=0, keepdims=True)
        sel = work == m
        if tie_rule:
            first = jnp.min(jnp.where(sel, order, big), axis=0, keepdims=True)
            sel = order == first
        rank = jnp.where(sel, float(r), rank)
        work = jnp.where(sel, -jnp.inf, work)
        if out_ref is not None:
            out_ref[r:r + 1, tok] = m
    return rank


def _column_max(x):
    slabs = [x[i:i + 8] for i in range(0, x.shape[0], 8)]
    while len(slabs) > 1:
        slabs = [jnp.maximum(slabs[i], slabs[i + 1]) if i + 1 < len(slabs) else slabs[i]
                 for i in range(0, len(slabs), 2)]
    return jnp.max(slabs[0], axis=0, keepdims=True)


def _pair_cells(op, u1, u2):
    g = [op(u1[0:1], u2[0:8]), op(u1[0:1], u2[8:16])]
    for a in range(1, 8):
        g.append(op(u1[a:a + 1], u2[0:8]))
    g.append(op(u1[8:16], u2[0:1]))
    return jnp.concatenate(g, axis=0)


def _topk_body(st_ref, lc_ref, f1_ref, r2_ref, f2_ref, t_ref, rk_ref, l_ref, z_ref, w2_ref):
    tl = st_ref.shape[-1]
    chunks = [slice(c * LANES, (c + 1) * LANES) for c in range(tl // LANES)]
    key_id = lax.broadcasted_iota(jnp.int32, (PEER_NKEYS, LANES), 0).astype(F32)

    def half_ranks(tie_rule):
        for tok in chunks:
            for p in range(2):
                rk_ref[p, :, tok] = _extract_top(st_ref[p, :, tok], key_id, tie_rule,
                                                 t_ref.at[p], tok)

    jobs = [(p, tok) for tok in chunks for p in range(2)]
    for j0 in range(0, len(jobs), FAST_CHAINS):
        group = jobs[j0:j0 + FAST_CHAINS]
        work = [st_ref[p, :, tok] for p, tok in group]
        for p, tok in group:
            rk_ref[p, :, tok] = jnp.full((PEER_NKEYS, LANES), float(PEER_TOPK), F32)
        for r in range(PEER_TOPK):
            for i, (p, tok) in enumerate(group):
                m = _column_max(work[i])
                sel = work[i] == m
                rk_ref[p, :, tok] = jnp.where(sel, float(r), rk_ref[p, :, tok])
                work[i] = jnp.where(sel, -jnp.inf, work[i])
                t_ref[p, r:r + 1, tok] = m
    taken = jnp.where(rk_ref[...] < float(PEER_TOPK), 1.0, 0.0)
    most = jnp.max(jnp.sum(taken, axis=1, keepdims=True))

    @pl.when(most > float(PEER_TOPK))
    def _():
        half_ranks(True)

    wide = 4 * LANES
    row = lax.broadcasted_iota(jnp.int32, (80, wide), 0)
    mid = row - 16
    cell_id = jnp.where(row < 16, row,
                        jnp.where(row < 72,
                                  (jnp.right_shift(mid, 3) + 1) * 16 + jnp.bitwise_and(mid, 7),
                                  (row - 64) * 16)).astype(F32)
    for c0 in range(0, tl, wide):
        tok2 = slice(c0, c0 + wide)
        t1 = t_ref[0, :, tok2]
        t2 = t_ref[1, :, tok2]
        work = _pair_cells(jnp.add, t1, t2)
        for _ in range(PEER_TOPK):
            work = jnp.where(work == _column_max(work), -jnp.inf, work)
        w2_ref[:, tok2] = work
        taken = jnp.sum(jnp.where(work == -jnp.inf, 1.0, 0.0), axis=0, keepdims=True)

        @pl.when(jnp.max(taken) > float(PEER_TOPK))
        def _():
            again = _pair_cells(jnp.add, t_ref[0, :, tok2], t_ref[1, :, tok2])
            for _ in range(PEER_TOPK):
                m = jnp.max(again, axis=0, keepdims=True)
                first = jnp.min(jnp.where(again == m, cell_id, float(PEER_N)), axis=0,
                                keepdims=True)
                again = jnp.where(cell_id == first, -jnp.inf, again)
            w2_ref[:, tok2] = again

        chosen = jnp.where(w2_ref[:, tok2] == -jnp.inf, 1.0, 0.0)
        e1 = jnp.exp(t1 - t1[0:1])
        e2 = jnp.exp(t2 - t2[0:1])
        z = jnp.sum(chosen * _pair_cells(jnp.multiply, e1, e2), axis=0, keepdims=True)
        z_ref[0:1, tok2] = 1.0 / z
        l_ref[0:1, tok2] = jnp.sum(chosen[0:16], axis=0, keepdims=True)
        for a in range(1, 8):
            l_ref[a:a + 1, tok2] = jnp.sum(chosen[8 + 8 * a:16 + 8 * a], axis=0, keepdims=True)
        l_ref[8:16, tok2] = chosen[72:80]
    for tok in chunks:
        rank1 = rk_ref[0, :, tok]
        rank2 = rk_ref[1, :, tok]
        lcnt = l_ref[:, tok]
        lc = jnp.zeros((PEER_NKEYS, LANES), F32)
        for a in range(PEER_TOPK):
            lc = jnp.where(rank1 == float(a), lcnt[a:a + 1], lc)
        lc_ref[:, tok] = lc
        f1_ref[:, tok] = jnp.exp(st_ref[0, :, tok] - t_ref[0, 0:1, tok]) * z_ref[0:1, tok]
        r2_ref[:, tok] = rank2
        f2_ref[:, tok] = jnp.where(rank2 < float(PEER_TOPK),
                                   jnp.exp(st_ref[1, :, tok] - t_ref[1, 0:1, tok]), 0.0)


def _topk(st):
    _, nk, t = st.shape
    tl = min(TOPK_TL, t)
    spec = pl.BlockSpec((None, nk, tl), lambda i, h: (h, 0, i))
    shp = jax.ShapeDtypeStruct((PEER_HEADS, nk, t), F32)
    return pl.pallas_call(
        _topk_body,
        grid=(t // tl, PEER_HEADS),
        in_specs=[pl.BlockSpec((2, nk, tl), lambda i, h: (h, 0, i))],
        out_specs=[spec, spec, spec, spec],
        out_shape=[shp, shp, shp, shp],
        scratch_shapes=[pltpu.VMEM((2, PEER_TOPK, tl), F32), pltpu.VMEM((2, nk, tl), F32),
                        pltpu.VMEM((PEER_TOPK, tl), F32), pltpu.VMEM((8, tl), F32),
                        pltpu.VMEM((80, tl), F32)],
        compiler_params=_cparams("arbitrary", "arbitrary"),
        name="topk",
    )(st)


def _peer_body(h2_ref, ed_ref, eut_ref, lc_ref, f1_ref, r2_ref, f2_ref, x1_ref, gf_ref,
               g2_ref, b2_ref, o_ref, acc_ref, *, ni):
    et = pl.program_id(1)

    @pl.when(et == 0)
    def _():
        acc_ref[...] = jnp.zeros(acc_ref.shape, F32)

    tm = h2_ref.shape[0]
    sub = BF16_SUBLANES
    kgroup = 2 * sub
    at = lax.dot_general(ed_ref[...], h2_ref[...], _NT, preferred_element_type=F32)
    cols = []
    for c in range(tm // LANES):
        tok = slice(c * LANES, (c + 1) * LANES)
        w_rows = {}
        for k0 in range(0, PEER_NKEYS, kgroup):
            subs = [slice(k0 + j * sub, k0 + (j + 1) * sub) for j in range(kgroup // sub)]
            rank = [[r2_ref[h, rk, tok].astype(BF16) for rk in subs] for h in range(PEER_HEADS)]
            val = [[f2_ref[h, rk, tok].astype(BF16) for rk in subs] for h in range(PEER_HEADS)]
            for ii in range(ni):
                g = [None] * len(subs)
                for h in range(PEER_HEADS):
                    cnt = jnp.broadcast_to(lc_ref[h, ii:ii + 1, tok], (sub, LANES)).astype(BF16)
                    fac = jnp.broadcast_to(f1_ref[h, ii:ii + 1, tok], (sub, LANES)).astype(BF16)
                    for j in range(len(subs)):
                        term = jnp.where(rank[h][j] < cnt, val[h][j], 0) * fac
                        g[j] = term if g[j] is None else g[j] + term
                for j, rk in enumerate(subs):
                    r0 = ii * PEER_NKEYS + rk.start
                    a = at[r0:r0 + sub, tok]
                    act = 0.5 * a * (1.0 + lax.erf(a * (2.0 ** -0.5)))
                    w_rows[r0] = g[j] * act.astype(BF16)
        cols.append(jnp.concatenate([w_rows[r0] for r0 in sorted(w_rows)], axis=0))
    wt = jnp.concatenate(cols, axis=1)
    acc_ref[...] += jnp.dot(eut_ref[...], wt, preferred_element_type=F32)

    @pl.when(et == pl.num_programs(1) - 1)
    def _():
        f = acc_ref[...].T
        y = DN_ALPHA * x1_ref[...] + (1.0 + gf_ref[...]) * f
        o_ref[...] = _ln(y) * g2_ref[...] + b2_ref[...]


def _peer(h2, ed, eut, lc, f1, r2, f2, x1, mod4, ln_g, ln_b, seq):
    t, d = h2.shape
    tm = min(PEER_TM, t)
    te = PEER_TE
    ni = te // PEER_NKEYS
    ne = PEER_N // te
    tpb = seq // tm
    lc4 = lc.reshape(PEER_HEADS, ne, ni, t)
    f14 = f1.reshape(PEER_HEADS, ne, ni, t)
    rowspec = pl.BlockSpec((PEER_HEADS, None, ni, tm), lambda i, e: (0, e, 0, i))
    colspec = pl.BlockSpec((PEER_HEADS, PEER_NKEYS, tm), lambda i, e: (0, 0, i))
    full = lambda *s: pl.BlockSpec(s, lambda i, e: (0,) * len(s))
    return pl.pallas_call(
        functools.partial(_peer_body, ni=ni),
        grid=(t // tm, ne),
        in_specs=[pl.BlockSpec((tm, d), lambda i, e: (i, 0)),
                  pl.BlockSpec((te, d), lambda i, e: (e, 0)),
                  pl.BlockSpec((d, te), lambda i, e: (0, e)),
                  rowspec, rowspec, colspec, colspec,
                  pl.BlockSpec((tm, d), lambda i, e: (i, 0)),
                  pl.BlockSpec((None, None, 1, d), lambda i, e: (i // tpb, 5, 0, 0)),
                  full(1, d), full(1, d)],
        out_specs=pl.BlockSpec((tm, d), lambda i, e: (i, 0)),
        out_shape=jax.ShapeDtypeStruct((t, d), F32),
        scratch_shapes=[pltpu.VMEM((d, tm), F32)],
        compiler_params=_cparams("arbitrary", "arbitrary"),
        name="peer",
    )(h2, ed, eut, lc4, f14, r2, f2, x1, mod4, ln_g, ln_b)


def kernel(x, c, w_ada, b_ada, w_in, w_out, na_rpb, t5_bias, lambda_q1, lambda_k1, lambda_q2,
           lambda_k2, diff_subln, ln1_g, ln1_b, w_query, sub_keys, expert_down, expert_up,
           ln2_g, ln2_b):
    b, seq, d = x.shape
    t = b * seq
    rows = seq // GRID_W
    assert d == D_MODEL and w_ada.shape[0] == DEPTH == 1
    assert rows % NA_QROWS == 0 and rows >= NA_KROWS and seq % DIFF_TQ == 0
    l = 0
    lambda_init = 0.8 - 0.6 * math.exp(-0.3 * l)
    xf = x.reshape(t, d)

    mod4 = _ada(c, w_ada[l], b_ada[l][None, :]).reshape(b, 6, 1, d)
    proj3 = _proj(xf, mod4, w_in[l].astype(BF16), seq).reshape(b, seq, IN_WIDTH)

    nb = _na_bias(na_rpb[l].reshape(-1), rows)
    o_na = _na(proj3, nb, rows).reshape(t, NA_WIDTH)

    dt = _t5_bias(t5_bias.reshape(-1), seq)
    o_df = _diff(proj3, dt, lambda_q1[l][None, :], lambda_k1[l][None, :], lambda_q2[l][None, :],
                 lambda_k2[l][None, :], diff_subln[l][None, :], lambda_init).reshape(t, DIFF_WIDTH)

    x1, h2, st = _mid(o_na, o_df, w_out[l].astype(BF16), xf, mod4, ln1_g[l][None, :],
                      ln1_b[l][None, :], w_query[l].astype(BF16), sub_keys[l].astype(BF16), seq)
    lc, f1, r2, f2 = _topk(st)
    out = _peer(h2, expert_down[l].astype(BF16), expert_up[l].astype(BF16).T, lc, f1, r2, f2,
                x1, mod4, ln2_g[l][None, :], ln2_b[l][None, :], seq)
    return out.reshape(b, seq, d)
```
